```python
import jax, jax.numpy as jnp
from jax import lax
import numpy as np

D_MODEL = 2048
BATCH = 2
SEQ = 4096
DEPTH = 2
DEC_BATCH = 8
DEC_SEQ = 1
PAST_LEN = 16384
PAGE_SIZE = 128

A_WINDOWS = (128, 512, 2048)
A_DILATIONS = (1, 4, 16)
N_A_GROUPS = 3
H_A = D_MODEL // 512
HD_A = 128
A_BLK = 128
A_W = H_A * HD_A
H_M = D_MODEL // 512
DK_M = 128
DV_M = 256
M_CHUNK = 64
M_QK = H_M * DK_M
M_V = H_M * DV_M
W_C = D_MODEL // 2
CONV_W = 3
D_FF = ((8 * D_MODEL + 3 * 256 - 1) // (3 * 256)) * 256
EPS = 1e-6

SPLIT_SIZES = (A_W,) * 9 + (M_QK, M_QK, M_V, M_V, H_M, H_M) + (W_C,) * 3 + (D_MODEL,) * 3
IN_WIDTH = int(sum(SPLIT_SIZES))
SPLIT_POINTS = tuple(int(v) for v in np.cumsum(SPLIT_SIZES)[:-1])

kernel_name = 'dilated_mlstm_shortconv_hybrid_step'

F32 = jnp.float32


def rmsnorm(x, g):
    xf = x.astype(F32)
    y = xf * lax.rsqrt(jnp.mean(xf * xf, axis=-1, keepdims=True) + EPS)
    return (y * g.astype(F32)).astype(x.dtype)


def modulate(x, g, shift, scale):
    return rmsnorm(x, g) * (1 + scale[:, None, :]) + shift[:, None, :]


def dilated_attn_prompt(q, k, v, dil, win):
    b, s, nh, dh = q.shape
    span = win // dil
    seg = dil * A_BLK
    s_pad = -(-s // seg) * seg
    nb = s_pad // seg
    pad = ((0, 0), (0, s_pad - s), (0, 0), (0, 0))

    def to_blocks(t):
        t = jnp.pad(t, pad).reshape(b, s_pad // dil, dil, nh, dh).transpose(0, 2, 1, 3, 4)
        return t.reshape(b, dil, nb, A_BLK, nh, dh)

    def with_prev(t):
        prev = jnp.pad(t, ((0, 0), (0, 0), (1, 0), (0, 0), (0, 0), (0, 0)))[:, :, :nb]
        return jnp.concatenate([prev, t], axis=3)

    qb, kb, vb = to_blocks(q), to_blocks(k), to_blocks(v)
    kk, vv = with_prev(kb), with_prev(vb)
    sc = jnp.einsum('brnqhd,brnkhd->brnhqk', qb, kk).astype(F32) * (dh ** -0.5)
    qi = jnp.arange(A_BLK)[:, None]
    ki = jnp.arange(2 * A_BLK)[None, :]
    dist = qi + A_BLK - ki
    band = (dist >= 0) & (dist <= span)
    real = (jnp.arange(nb)[:, None, None] > 0) | (ki[None] >= A_BLK)
    valid = band[None] & real
    sc = jnp.where(valid[None, None, :, None], sc, -jnp.inf)
    lse = jax.nn.logsumexp(sc, axis=-1)
    p = jnp.exp(sc - lse[..., None])
    o = jnp.einsum('brnhqk,brnkhd->brnqhd', p.astype(vv.dtype), vv)
    o = o.reshape(b, dil, s_pad // dil, nh, dh).transpose(0, 2, 1, 3, 4).reshape(b, s_pad, nh, dh)[:, :s]
    lse = lse.transpose(0, 1, 2, 4, 3).reshape(b, dil, s_pad // dil, nh)
    lse = lse.transpose(0, 2, 1, 3).reshape(b, s_pad, nh)[:, :s]
    return o, lse


def dilated_attn_step(q, kv_buf, kv_new, dil, win):
    lg = kv_buf.shape[1]
    ds = q.shape[1]
    dh = q.shape[-1]
    kv_all = jnp.concatenate([kv_buf, kv_new.astype(kv_buf.dtype)], axis=1)
    rows = lg + jnp.arange(ds)[:, None] - dil * jnp.arange(win // dil + 1)[None, :]
    valid = rows >= 0
    sel = kv_all[:, jnp.maximum(rows, 0)]
    sc = jnp.einsum('bshd,bsjhd->bshj', q, sel[:, :, :, 0].astype(q.dtype)).astype(F32) * (dh ** -0.5)
    sc = jnp.where(valid[None, :, None, :], sc, -jnp.inf)
    lse = jax.nn.logsumexp(sc, axis=-1)
    p = jnp.exp(sc - lse[..., None])
    o = jnp.einsum('bshj,bsjhd->bshd', p.astype(q.dtype), sel[:, :, :, 1].astype(q.dtype))
    return o, lse, kv_all[:, ds:]


def merge_dilations(outs, lses):
    alpha = jax.nn.softmax(jnp.stack(lses, axis=0), axis=0)
    o = jnp.stack([t.astype(F32) for t in outs], axis=0)
    return jnp.sum(alpha[..., None] * o, axis=0)


def mlstm_chunk(carry, inp):
    c_st, n_st, m_st = carry
    q, k, v, ig, lf = inp
    L = q.shape[1]
    bcum = jnp.cumsum(lf, axis=1).transpose(0, 2, 1)
    igt = ig.transpose(0, 2, 1)
    causal = jnp.tril(jnp.ones((L, L), dtype=bool))
    dlog = jnp.where(causal, bcum[..., :, None] - bcum[..., None, :] + igt[..., None, :], -jnp.inf)
    g = bcum + m_st[..., None]
    m_row = jnp.maximum(g, jnp.max(dlog, axis=-1))
    w_intra = jnp.exp(dlog - m_row[..., None])
    w_inter = jnp.exp(g - m_row)
    s = jnp.einsum('bthk,bshk->bhts', q, k) * w_intra
    num = jnp.einsum('bhts,bshv->bhtv', s, v) + w_inter[..., None] * jnp.einsum('bthk,bhkv->bhtv', q, c_st)
    den = jnp.sum(s, axis=-1) + w_inter * jnp.einsum('bthk,bhk->bht', q, n_st)
    h = num / jnp.maximum(jnp.abs(den), jnp.exp(-m_row))[..., None]
    b_last = bcum[..., -1]
    wlog = b_last[..., None] - bcum + igt
    g_last = b_last + m_st
    m_new = jnp.maximum(g_last, jnp.max(wlog, axis=-1))
    w_s = jnp.exp(wlog - m_new[..., None])
    decay = jnp.exp(g_last - m_new)
    c_new = decay[..., None, None] * c_st + jnp.einsum('bhs,bshk,bshv->bhkv', w_s, k, v)
    n_new = decay[..., None] * n_st + jnp.einsum('bhs,bshk->bhk', w_s, k)
    return (c_new, n_new, m_new), h.transpose(0, 2, 1, 3)


def token_mixing(h, w_in, b_igate, b_fgate, g_mlstm, conv_w, w_br_a, w_br_m, w_br_c, w_mix_out, state):
    b, s, _ = h.shape
    parts = jnp.split(h @ w_in, SPLIT_POINTS, axis=-1)
    a_parts, m_parts, c_parts, gate_parts = parts[:9], parts[9:15], parts[15:18], parts[18:21]
    outs, lses, bufs = [], [], []
    for gi in range(N_A_GROUPS):
        q, k, v = [t.reshape(b, s, H_A, HD_A) for t in a_parts[3 * gi:3 * gi + 3]]
        kv = jnp.stack([k, v], axis=2)
        if state is None:
            o, lse = dilated_attn_prompt(q, k, v, A_DILATIONS[gi], A_WINDOWS[gi])
            keep = min(A_WINDOWS[gi], s)
            buf = kv[:, s - keep:]
        else:
            o, lse, buf = dilated_attn_step(q, state[0][gi], kv, A_DILATIONS[gi], A_WINDOWS[gi])
        outs.append(o)
        lses.append(lse)
        bufs.append(buf)
    a_out = merge_dilations(outs, lses).reshape(b, s, A_W).astype(h.dtype)
    mq, mk, mv, mo, mi, mf = m_parts
    q = mq.reshape(b, s, H_M, DK_M).astype(F32)
    k = mk.reshape(b, s, H_M, DK_M).astype(F32) * (DK_M ** -0.5)
    v = mv.reshape(b, s, H_M, DV_M).astype(F32)
    ig = mi.astype(F32) + b_igate.astype(F32)
    lf = jax.nn.log_sigmoid(mf.astype(F32) + b_fgate.astype(F32))
    if state is None:
        nc = s // M_CHUNK
        carry0 = (jnp.zeros((b, H_M, DK_M, DV_M), F32), jnp.zeros((b, H_M, DK_M), F32), jnp.zeros((b, H_M), F32))
        to_chunks = lambda t: t.reshape((b, nc, M_CHUNK) + t.shape[2:]).swapaxes(0, 1)
        (c_n, n_n, m_n), hs = lax.scan(mlstm_chunk, carry0, (to_chunks(q), to_chunks(k), to_chunks(v), to_chunks(ig), to_chunks(lf)))
        hm = hs.swapaxes(0, 1).reshape(b, s, H_M, DV_M)
    else:
        c0, n0, m0 = state[1]
        (c_n, n_n, m_n), hm = mlstm_chunk((c0.astype(F32), n0.astype(F32), m0.astype(F32)), (q, k, v, ig, lf))
    hm = hm * lax.rsqrt(jnp.mean(hm * hm, axis=-1, keepdims=True) + EPS) * g_mlstm.astype(F32).reshape(H_M, DV_M)
    hm = (hm.reshape(b, s, M_V) * jax.nn.sigmoid(mo.astype(F32))).astype(h.dtype)
    u, bg, cg = c_parts
    p = cg * u
    prev = jnp.zeros((b, CONV_W - 1, W_C), p.dtype) if state is None else state[2].astype(p.dtype)
    pp = jnp.concatenate([prev, p], axis=1)
    z = conv_w[0] * pp[:, 0:s]
    for j in range(1, CONV_W):
        z = z + conv_w[j] * pp[:, j:j + s]
    c_out = bg * z
    new_conv = pp[:, pp.shape[1] - (CONV_W - 1):]
    ga, gm, gc = [jax.nn.sigmoid(t) for t in gate_parts]
    merged = ga * (a_out @ w_br_a) + gm * (hm @ w_br_m) + gc * (c_out @ w_br_c)
    return merged @ w_mix_out, [bufs[0], bufs[1], bufs[2], c_n, n_n, m_n, new_conv]


def run_group(x, c, states, w_ada, b_ada, g_norm1, g_norm2, w_in, b_igate, b_fgate, g_mlstm, conv_w,
              w_br_a, w_br_m, w_br_c, w_mix_out, w_ffn_gate, w_ffn_up, w_ffn_down, g_final):
    new = []
    for l in range(DEPTH):
        sh1, sc1, gt1, sh2, sc2, gt2 = jnp.split(jax.nn.silu(c) @ w_ada[l] + b_ada[l], 6, axis=-1)
        h = modulate(x, g_norm1[l], sh1, sc1)
        st = None if states is None else (
            (states[0][l], states[1][l], states[2][l]), (states[3][l], states[4][l], states[5][l]), states[6][l])
        mix, new_l = token_mixing(h, w_in[l], b_igate[l], b_fgate[l], g_mlstm[l], conv_w[l],
                                  w_br_a[l], w_br_m[l], w_br_c[l], w_mix_out[l], st)
        x = x + gt1[:, None, :] * mix
        h2 = modulate(x, g_norm2[l], sh2, sc2)
        ffn = (jax.nn.silu(h2 @ w_ffn_gate[l]) * (h2 @ w_ffn_up[l])) @ w_ffn_down[l]
        x = x + gt2[:, None, :] * ffn
        new.append(new_l)
    y = rmsnorm(x, g_final)
    stacked = [jnp.stack([n[i] for n in new], axis=0) for i in range(7)]
    return y, stacked


def setup_inputs(seed: int = 0) -> dict:
    key = jax.random.key(seed)
    ks = jax.random.split(key, 32)
    nrm = lambda k, shape, sc: jax.random.normal(k, shape, F32) * sc
    lens = [min(w, PAST_LEN) for w in A_WINDOWS]
    return {
        'x_prompt': nrm(ks[0], (BATCH, SEQ, D_MODEL), 1.0),
        'x_sample': nrm(ks[1], (DEC_BATCH, DEC_SEQ, D_MODEL), 1.0),
        'cache_a1_kv': nrm(ks[2], (DEPTH, DEC_BATCH, lens[0], 2, H_A, HD_A), 1.0),
        'cache_a2_kv': nrm(ks[3], (DEPTH, DEC_BATCH, lens[1], 2, H_A, HD_A), 1.0),
        'cache_a3_kv': nrm(ks[4], (DEPTH, DEC_BATCH, lens[2], 2, H_A, HD_A), 1.0),
        'state_mlstm_c': nrm(ks[5], (DEPTH, DEC_BATCH, H_M, DK_M, DV_M), 0.5),
        'state_mlstm_n': nrm(ks[6], (DEPTH, DEC_BATCH, H_M, DK_M), 0.5),
        'state_mlstm_m': nrm(ks[7], (DEPTH, DEC_BATCH, H_M), 1.0),
        'state_conv': nrm(ks[8], (DEPTH, DEC_BATCH, CONV_W - 1, W_C), 1.0),
        'c_prompt': nrm(ks[9], (BATCH, D_MODEL), 1.0),
        'c_sample': nrm(ks[10], (DEC_BATCH, D_MODEL), 1.0),
        'w_ada': nrm(ks[11], (DEPTH, D_MODEL, 6 * D_MODEL), 0.5 * D_MODEL ** -0.5),
        'b_ada': nrm(ks[12], (DEPTH, 6 * D_MODEL), 0.02),
        'g_norm1': 1.0 + nrm(ks[13], (DEPTH, D_MODEL), 0.02),
        'g_norm2': 1.0 + nrm(ks[14], (DEPTH, D_MODEL), 0.02),
        'w_in': nrm(ks[15], (DEPTH, D_MODEL, IN_WIDTH), D_MODEL ** -0.5),
        'b_igate': nrm(ks[16], (DEPTH, H_M), 0.5),
        'b_fgate': 3.0 + nrm(ks[17], (DEPTH, H_M), 0.5),
        'g_mlstm': 1.0 + nrm(ks[18], (DEPTH, M_V), 0.02),
        'conv_w': nrm(ks[19], (DEPTH, CONV_W, W_C), CONV_W ** -0.5),
        'w_br_a': nrm(ks[20], (DEPTH, A_W, D_MODEL), A_W ** -0.5),
        'w_br_m': nrm(ks[21], (DEPTH, M_V, D_MODEL), M_V ** -0.5),
        'w_br_c': nrm(ks[22], (DEPTH, W_C, D_MODEL), W_C ** -0.5),
        'w_mix_out': nrm(ks[23], (DEPTH, D_MODEL, D_MODEL), D_MODEL ** -0.5),
        'w_ffn_gate': nrm(ks[24], (DEPTH, D_MODEL, D_FF), D_MODEL ** -0.5),
        'w_ffn_up': nrm(ks[25], (DEPTH, D_MODEL, D_FF), D_MODEL ** -0.5),
        'w_ffn_down': nrm(ks[26], (DEPTH, D_FF, D_MODEL), D_FF ** -0.5),
        'g_final': 1.0 + nrm(ks[27], (D_MODEL,), 0.02),
    }


def reference(x_prompt, x_sample, cache_a1_kv, cache_a2_kv, cache_a3_kv, state_mlstm_c, state_mlstm_n,
              state_mlstm_m, state_conv, c_prompt, c_sample, w_ada, b_ada, g_norm1, g_norm2, w_in, b_igate,
              b_fgate, g_mlstm, conv_w, w_br_a, w_br_m, w_br_c, w_mix_out, w_ffn_gate, w_ffn_up, w_ffn_down,
              g_final):
    y_prompt, ps = run_group(x_prompt, c_prompt, None, w_ada, b_ada, g_norm1, g_norm2, w_in, b_igate, b_fgate,
                             g_mlstm, conv_w, w_br_a, w_br_m, w_br_c, w_mix_out, w_ffn_gate, w_ffn_up,
                             w_ffn_down, g_final)
    states = (cache_a1_kv, cache_a2_kv, cache_a3_kv, state_mlstm_c, state_mlstm_n, state_mlstm_m, state_conv)
    y_sample, ss = run_group(x_sample, c_sample, states, w_ada, b_ada, g_norm1, g_norm2, w_in, b_igate, b_fgate,
                             g_mlstm, conv_w, w_br_a, w_br_m, w_br_c, w_mix_out, w_ffn_gate, w_ffn_up,
                             w_ffn_down, g_final)
    return (y_prompt, y_sample, ps[0], ss[0], ps[1], ss[1], ps[2], ss[2], ps[3], ss[3], ps[4], ss[4],
            ps[5], ss[5], ps[6], ss[6])
```

```python
import functools

import jax
import jax.numpy as jnp
from jax import lax
from jax.experimental import pallas as pl
from jax.experimental.pallas import tpu as pltpu

F32 = jnp.float32
BF16 = jnp.bfloat16
EPS = 1e-6

A_WINDOWS = (128, 512, 2048)
A_DILATIONS = (1, 4, 16)
N_GROUPS = 3
HD = 128
A_BLK = 128
N_HEADS = 4
DV = 256
A_W = N_HEADS * HD
M_QK = N_HEADS * HD
M_V = N_HEADS * DV
W_C = 1024
CONV_W = 3
U_MAIN = 9 * A_W + 2 * M_QK + 2 * M_V
N_GATE_COLS = 2 * N_HEADS
U_REST_OFF = U_MAIN + N_GATE_COLS
LANES = 128

VMEM_LIMIT = 52 * 1024 * 1024


def _cparams(sem):
    return pltpu.CompilerParams(dimension_semantics=sem, vmem_limit_bytes=VMEM_LIMIT)


def _ada_kernel(c_ref, w_ref, b_ref, o_ref):
    c = c_ref[...]
    a = (c * jax.nn.sigmoid(c)).astype(BF16)
    o_ref[...] = jnp.dot(a, w_ref[...].astype(BF16), preferred_element_type=F32) + b_ref[...]


def _ada(c_all, w_ada, b_ada, tn=1024):
    depth, d, n = w_ada.shape
    r = c_all.shape[0]
    return pl.pallas_call(
        _ada_kernel,
        grid=(depth, n // tn),
        in_specs=[
            pl.BlockSpec((r, d), lambda l, j: (0, 0)),
            pl.BlockSpec((None, d, tn), lambda l, j: (l, 0, j)),
            pl.BlockSpec((None, 1, tn), lambda l, j: (l, 0, j)),
        ],
        out_specs=pl.BlockSpec((None, r, tn), lambda l, j: (l, 0, j)),
        out_shape=jax.ShapeDtypeStruct((depth, r, n), F32),
        compiler_params=_cparams(("parallel", "parallel")),
        name="ada",
    )(c_all, w_ada, b_ada.reshape(depth, 1, n))


def _modulated_norm(x_ref, g_ref, sh_ref, sc_ref, h_scr):
    x = x_ref[...]
    ms = jnp.mean(x * x, axis=-1, keepdims=True)
    y = x * lax.rsqrt(ms + EPS) * g_ref[...]
    h_scr[...] = (y * (1.0 + sc_ref[...]) + sh_ref[...]).astype(BF16)


def _in_main_kernel(x_ref, g_ref, sh_ref, sc_ref, w_ref, wg_ref, o_ref, og_ref, h_scr):
    @pl.when(pl.program_id(1) == 0)
    def _():
        _modulated_norm(x_ref, g_ref, sh_ref, sc_ref, h_scr)
        og_ref[...] = jnp.dot(h_scr[...], wg_ref[...].astype(BF16), preferred_element_type=F32)

    o_ref[...] = jnp.dot(h_scr[...], w_ref[...].astype(BF16),
                         preferred_element_type=F32).astype(o_ref.dtype)


def _in_rest_kernel(x_ref, g_ref, sh_ref, sc_ref, w_ref, o_ref, h_scr):
    @pl.when(pl.program_id(1) == 0)
    def _():
        _modulated_norm(x_ref, g_ref, sh_ref, sc_ref, h_scr)

    o_ref[...] = jnp.dot(h_scr[...], w_ref[...].astype(BF16),
                         preferred_element_type=F32).astype(o_ref.dtype)


def _ffn_up_kernel(x_ref, g_ref, sh_ref, sc_ref, wg_ref, wu_ref, o_ref, h_scr):
    @pl.when(pl.program_id(1) == 0)
    def _():
        _modulated_norm(x_ref, g_ref, sh_ref, sc_ref, h_scr)

    h = h_scr[...]
    a = jnp.dot(h, wg_ref[...].astype(BF16), preferred_element_type=F32)
    b = jnp.dot(h, wu_ref[...].astype(BF16), preferred_element_type=F32)
    o_ref[...] = (a * jax.nn.sigmoid(a) * b).astype(o_ref.dtype)


def _row_specs(m, d, tm, g_norm, shift, scale):
    groups = shift.shape[0]
    blocks_per_group = (m // groups) // tm
    mod = pl.BlockSpec((None, shift.shape[1], d), lambda i, j: (i // blocks_per_group, 0, 0))
    return [
        pl.BlockSpec((tm, d), lambda i, j: (i, 0)),
        pl.BlockSpec((1, d), lambda i, j: (0, 0)),
        mod,
        mod,
    ]


def _in_proj(x, g_norm, shift, scale, w_in, layer, w_gate_pad, w_rest, tm, tn=512):
    m, d = x.shape
    g2 = g_norm.reshape(1, d)
    rows = _row_specs(m, d, tm, g2, shift, scale)
    u_main, gates = pl.pallas_call(
        _in_main_kernel,
        grid=(m // tm, U_MAIN // tn),
        in_specs=rows + [
            pl.BlockSpec((None, d, tn), lambda i, j: (layer, 0, j)),
            pl.BlockSpec((d, LANES), lambda i, j: (0, 0)),
        ],
        out_specs=[
            pl.BlockSpec((tm, tn), lambda i, j: (i, j)),
            pl.BlockSpec((tm, LANES), lambda i, j: (i, 0)),
        ],
        out_shape=[
            jax.ShapeDtypeStruct((m, U_MAIN), BF16),
            jax.ShapeDtypeStruct((m, LANES), F32),
        ],
        scratch_shapes=[pltpu.VMEM((tm, d), BF16)],
        compiler_params=_cparams(("parallel", "arbitrary")),
        name="in_main",
    )(x, g2, shift, scale, w_in, w_gate_pad)
    n_rest = w_rest.shape[1]
    u_rest = pl.pallas_call(
        _in_rest_kernel,
        grid=(m // tm, n_rest // tn),
        in_specs=rows + [pl.BlockSpec((d, tn), lambda i, j: (0, j))],
        out_specs=pl.BlockSpec((tm, tn), lambda i, j: (i, j)),
        out_shape=jax.ShapeDtypeStruct((m, n_rest), BF16),
        scratch_shapes=[pltpu.VMEM((tm, d), BF16)],
        compiler_params=_cparams(("parallel", "arbitrary")),
        name="in_rest",
    )(x, g2, shift, scale, w_rest)
    return u_main, gates, u_rest


def _ffn_up(x, g_norm, shift, scale, w_gate, w_up, layer, tm, tn=512):
    m, d = x.shape
    d_ff = w_gate.shape[-1]
    g2 = g_norm.reshape(1, d)
    wspec = pl.BlockSpec((None, d, tn), lambda i, j: (layer, 0, j))
    return pl.pallas_call(
        _ffn_up_kernel,
        grid=(m // tm, d_ff // tn),
        in_specs=_row_specs(m, d, tm, g2, shift, scale) + [wspec, wspec],
        out_specs=pl.BlockSpec((tm, tn), lambda i, j: (i, j)),
        out_shape=jax.ShapeDtypeStruct((m, d_ff), BF16),
        scratch_shapes=[pltpu.VMEM((tm, d), BF16)],
        compiler_params=_cparams(("parallel", "arbitrary")),
        name="ffn_up",
    )(x, g2, shift, scale, w_gate, w_up)


def _mm_res_kernel(final, a_ref, w_ref, res_ref, gt_ref, *rest):
    if final:
        gf_ref, o_ref, acc = rest
    else:
        o_ref, acc = rest
    k = pl.program_id(1)

    @pl.when(k == 0)
    def _():
        acc[...] = jnp.zeros_like(acc)

    acc[...] += jnp.dot(a_ref[...], w_ref[...].astype(BF16), preferred_element_type=F32)

    @pl.when(k == pl.num_programs(1) - 1)
    def _():
        x = res_ref[...] + gt_ref[...] * acc[...]
        if final:
            ms = jnp.mean(x * x, axis=-1, keepdims=True)
            x = x * lax.rsqrt(ms + EPS) * gf_ref[...]
        o_ref[...] = x


def _mm_res(a, w, layer, res, gate, g_final, tm, tk=512):
    m, kdim = a.shape
    d = res.shape[-1]
    groups = gate.shape[0]
    blocks_per_group = (m // groups) // tm
    final = g_final is not None
    in_specs = [
        pl.BlockSpec((tm, tk), lambda i, k: (i, k)),
        pl.BlockSpec((None, tk, d), lambda i, k: (layer, k, 0)),
        pl.BlockSpec((tm, d), lambda i, k: (i, 0)),
        pl.BlockSpec((None, gate.shape[1], d), lambda i, k: (i // blocks_per_group, 0, 0)),
    ]
    args = [a, w, res, gate]
    if final:
        in_specs.append(pl.BlockSpec((1, d), lambda i, k: (0, 0)))
        args.append(g_final.reshape(1, d))
    return pl.pallas_call(
        functools.partial(_mm_res_kernel, final),
        grid=(m // tm, kdim // tk),
        in_specs=in_specs,
        out_specs=pl.BlockSpec((tm, d), lambda i, k: (i, 0)),
        out_shape=jax.ShapeDtypeStruct((m, d), F32),
        scratch_shapes=[pltpu.VMEM((tm, d), F32)],
        compiler_params=_cparams(("parallel", "arbitrary")),
        name="mm_res_final" if final else "mm_res",
    )(*args)


def _branch_kernel(a_ref, m_ref, c_ref, ga_ref, gm_ref, gc_ref, wa_ref, wm_ref, wc_ref, o_ref):
    def term(x_ref, w_ref, g_ref):
        y = jnp.dot(x_ref[...], w_ref[...].astype(BF16), preferred_element_type=F32)
        return jax.nn.sigmoid(g_ref[...].astype(F32)) * y

    o_ref[...] = (term(a_ref, wa_ref, ga_ref) + term(m_ref, wm_ref, gm_ref)
                  + term(c_ref, wc_ref, gc_ref)).astype(o_ref.dtype)


def _branch_merge(a_out, hm, c_out, u_rest, w_br_a, w_br_m, w_br_c, layer, tm, tn=512):
    m = a_out.shape[0]
    d = w_br_a.shape[-1]
    gate0 = (3 * W_C) // tn
    per_gate = d // tn

    def gate_spec(g):
        return pl.BlockSpec((tm, tn), lambda i, j: (i, gate0 + g * per_gate + j))

    def w_spec(k):
        return pl.BlockSpec((None, k, tn), lambda i, j: (layer, 0, j))

    return pl.pallas_call(
        _branch_kernel,
        grid=(m // tm, d // tn),
        in_specs=[
            pl.BlockSpec((tm, A_W), lambda i, j: (i, 0)),
            pl.BlockSpec((tm, M_V), lambda i, j: (i, 0)),
            pl.BlockSpec((tm, W_C), lambda i, j: (i, 0)),
            gate_spec(0), gate_spec(1), gate_spec(2),
            w_spec(A_W), w_spec(M_V), w_spec(W_C),
        ],
        out_specs=pl.BlockSpec((tm, tn), lambda i, j: (i, j)),
        out_shape=jax.ShapeDtypeStruct((m, d), BF16),
        compiler_params=_cparams(("parallel", "parallel")),
        name="branch_merge",
    )(a_out, hm, c_out, u_rest, u_rest, u_rest, w_br_a, w_br_m, w_br_c)


def _attn_kernel(q_ref, kp_ref, kc_ref, vp_ref, vc_ref, o_ref, lse_ref):
    n = pl.program_id(1)
    qi = lax.broadcasted_iota(jnp.int32, (A_BLK, A_BLK), 0)
    ki = lax.broadcasted_iota(jnp.int32, (A_BLK, A_BLK), 1)
    mask_cur = ki <= qi
    mask_prev = (ki - qi) >= jnp.where(n > 0, 0, A_BLK)
    scale = HD ** -0.5
    nt = (((1,), (1,)), ((), ()))
    lse_tile = jnp.zeros((A_BLK, LANES), F32)
    for h in range(N_HEADS):
        sl = slice(h * HD, (h + 1) * HD)
        q = q_ref[:, sl]
        s_cur = lax.dot_general(q, kc_ref[:, sl], nt, preferred_element_type=F32) * scale
        s_prev = lax.dot_general(q, kp_ref[:, sl], nt, preferred_element_type=F32) * scale
        s_cur = jnp.where(mask_cur, s_cur, -jnp.inf)
        s_prev = jnp.where(mask_prev, s_prev, -jnp.inf)
        mx = jnp.maximum(jnp.max(s_cur, axis=-1, keepdims=True),
                         jnp.max(s_prev, axis=-1, keepdims=True))
        p_cur = jnp.exp(s_cur - mx)
        p_prev = jnp.exp(s_prev - mx)
        den = jnp.sum(p_cur, axis=-1, keepdims=True) + jnp.sum(p_prev, axis=-1, keepdims=True)
        o = (jnp.dot(p_cur.astype(BF16), vc_ref[:, sl], preferred_element_type=F32)
             + jnp.dot(p_prev.astype(BF16), vp_ref[:, sl], preferred_element_type=F32))
        o_ref[:, sl] = (o / den).astype(o_ref.dtype)
        lse_tile = jnp.where(ki == h, mx + jnp.log(den), lse_tile)
    lse_ref[...] = lse_tile


def _attn_prompt(u_main, batch, seq, group):
    dil = A_DILATIONS[group]
    assert A_WINDOWS[group] // dil == A_BLK
    assert seq % (dil * A_BLK) == 0
    sub = seq // dil
    nb = sub // A_BLK
    col_blocks = U_MAIN // A_W
    uv = u_main.reshape(batch, sub, dil * U_MAIN)

    def spec(part, prev):
        if prev:
            return pl.BlockSpec((None, A_BLK, A_W),
                                lambda b, n, r: (b, jnp.maximum(n - 1, 0), r * col_blocks + 3 * group + part))
        return pl.BlockSpec((None, A_BLK, A_W),
                            lambda b, n, r: (b, n, r * col_blocks + 3 * group + part))

    o, lse = pl.pallas_call(
        _attn_kernel,
        grid=(batch, nb, dil),
        in_specs=[spec(0, False), spec(1, True), spec(1, False), spec(2, True), spec(2, False)],
        out_specs=[
            pl.BlockSpec((None, A_BLK, A_W), lambda b, n, r: (b, n, r)),
            pl.BlockSpec((None, A_BLK, LANES), lambda b, n, r: (b, n, r)),
        ],
        out_shape=[
            jax.ShapeDtypeStruct((batch, sub, dil * A_W), BF16),
            jax.ShapeDtypeStruct((batch, sub, dil * LANES), F32),
        ],
        compiler_params=_cparams(("parallel", "parallel", "parallel")),
        name=f"attn_g{group}",
    )(uv, uv, uv, uv, uv)
    return o.reshape(batch * seq, A_W), lse.reshape(batch * seq, LANES)


def _merge_groups(o_list, lse_list):
    mx = jnp.maximum(jnp.maximum(lse_list[0], lse_list[1]), lse_list[2])
    e = [jnp.exp(l - mx) for l in lse_list]
    den = e[0] + e[1] + e[2]
    alpha = [t / den for t in e]
    outs = []
    for h in range(N_HEADS):
        sl = slice(h * HD, (h + 1) * HD)
        acc = alpha[0][:, h:h + 1] * o_list[0][:, sl]
        acc += alpha[1][:, h:h + 1] * o_list[1][:, sl]
        acc += alpha[2][:, h:h + 1] * o_list[2][:, sl]
        outs.append(acc)
    return outs


def _mid_prompt_kernel(blocks_per_seq, o1_ref, o2_ref, o3_ref, l1_ref, l2_ref, l3_ref,
                       u_ref, b_ref, c_ref, cw_ref, a_ref, co_ref, st_ref, p_scr):
    tm = u_ref.shape[0]
    i = pl.program_id(0)
    merged = _merge_groups([o1_ref[...].astype(F32), o2_ref[...].astype(F32), o3_ref[...].astype(F32)],
                           [l1_ref[...], l2_ref[...], l3_ref[...]])
    for h in range(N_HEADS):
        a_ref[:, h * HD:(h + 1) * HD] = merged[h].astype(a_ref.dtype)

    @pl.when(i % blocks_per_seq == 0)
    def _():
        p_scr[0:8, :] = jnp.zeros((8, p_scr.shape[1]), F32)

    @pl.when(i % blocks_per_seq != 0)
    def _():
        p_scr[0:8, :] = p_scr[tm:tm + 8, :]

    p = c_ref[...].astype(F32) * u_ref[...].astype(F32)
    p_scr[8:8 + tm, :] = p
    z = (cw_ref[0:1, :] * p_scr[6:6 + tm, :] + cw_ref[1:2, :] * p_scr[7:7 + tm, :]
         + cw_ref[2:3, :] * p)
    co_ref[...] = (b_ref[...].astype(F32) * z).astype(co_ref.dtype)
    st_ref[...] = p[tm - (CONV_W - 1):tm, :]


def _mid_prompt(o_list, lse_list, u_rest, conv_w, batch, seq, tm=512):
    m = batch * seq
    blocks_per_seq = seq // tm
    row = lambda w: pl.BlockSpec((tm, w), lambda i: (i, 0))
    return pl.pallas_call(
        functools.partial(_mid_prompt_kernel, blocks_per_seq),
        grid=(m // tm,),
        in_specs=[row(A_W)] * 3 + [row(LANES)] * 3 + [
            pl.BlockSpec((tm, W_C), lambda i: (i, 0)),
            pl.BlockSpec((tm, W_C), lambda i: (i, 1)),
            pl.BlockSpec((tm, W_C), lambda i: (i, 2)),
            pl.BlockSpec((CONV_W, W_C), lambda i: (0, 0)),
        ],
        out_specs=[
            row(A_W), row(W_C),
            pl.BlockSpec((None, CONV_W - 1, W_C), lambda i: (i // blocks_per_seq, 0, 0)),
        ],
        out_shape=[
            jax.ShapeDtypeStruct((m, A_W), BF16),
            jax.ShapeDtypeStruct((m, W_C), BF16),
            jax.ShapeDtypeStruct((batch, CONV_W - 1, W_C), F32),
        ],
        scratch_shapes=[pltpu.VMEM((tm + 8, W_C), F32)],
        compiler_params=_cparams(("arbitrary",)),
        name="mid_prompt",
    )(*o_list, *lse_list, u_rest, u_rest, u_rest, conv_w)


M_CHUNK = 128


def _log_sigmoid(x):
    return jnp.minimum(x, 0.0) - jnp.log1p(jnp.exp(-jnp.abs(x)))


def _mlstm_kernel(ua_ref, ub_ref, gt_ref, bias_ref, gm_ref, hm_ref, c_ref, n_ref, m_ref):
    L = M_CHUNK

    @pl.when(pl.program_id(1) == 0)
    def _():
        c_ref[...] = jnp.zeros_like(c_ref)
        n_ref[...] = jnp.zeros_like(n_ref)
        m_ref[...] = jnp.zeros_like(m_ref)

    lane = lax.broadcasted_iota(jnp.int32, (L, LANES), 1)
    row = lax.broadcasted_iota(jnp.int32, (L, L), 0)
    col = lax.broadcasted_iota(jnp.int32, (L, L), 1)
    causal = col <= row
    scale = HD ** -0.5

    g = gt_ref[...] + bias_ref[...]
    x = jnp.where(lane < N_HEADS, g, _log_sigmoid(g))
    cum = jnp.dot(causal.astype(F32), x, precision=lax.Precision.HIGHEST,
                  preferred_element_type=F32)
    x_t = x.T
    cum_t = cum.T
    m_all = m_ref[...]
    m_tile = jnp.zeros((1, LANES), F32)
    lane1 = lax.broadcasted_iota(jnp.int32, (1, LANES), 1)
    nt = (((1,), (1,)), ((), ()))
    tn = (((0,), (0,)), ((), ()))
    for h in range(N_HEADS):
        ig_row = x_t[h:h + 1, :]
        ig_col = x[:, h:h + 1]
        b_row = cum_t[N_HEADS + h:N_HEADS + h + 1, :]
        b_col = cum[:, N_HEADS + h:N_HEADS + h + 1]
        m_st = m_all[:, h:h + 1]
        dlog = jnp.where(causal, b_col - b_row + ig_row, -jnp.inf)
        g_col = b_col + m_st
        m_row = jnp.maximum(g_col, jnp.max(dlog, axis=-1, keepdims=True))
        w_intra = jnp.exp(dlog - m_row)
        w_inter = jnp.exp(g_col - m_row)
        q = ua_ref[:, h * HD:(h + 1) * HD]
        k = ua_ref[:, M_QK + h * HD:M_QK + (h + 1) * HD]
        if h < 2:
            v = ua_ref[:, 2 * M_QK + h * DV:2 * M_QK + (h + 1) * DV]
        else:
            v = ub_ref[:, (h - 2) * DV:(h - 1) * DV]
        og = ub_ref[:, 2 * DV + h * DV:2 * DV + (h + 1) * DV]
        c_st = c_ref[h]
        n_st = n_ref[h:h + 1, :]
        s = lax.dot_general(q, k, nt, preferred_element_type=F32) * (w_intra * scale)
        num = (jnp.dot(s.astype(BF16), v, preferred_element_type=F32)
               + w_inter * jnp.dot(q, c_st.astype(BF16), preferred_element_type=F32))
        den = (jnp.sum(s, axis=-1, keepdims=True)
               + w_inter * jnp.sum(q.astype(F32) * n_st, axis=-1, keepdims=True))
        hh = num / jnp.maximum(jnp.abs(den), jnp.exp(-m_row))
        hn = hh * lax.rsqrt(jnp.mean(hh * hh, axis=-1, keepdims=True) + EPS) * gm_ref[:, h * DV:(h + 1) * DV]
        hm_ref[:, h * DV:(h + 1) * DV] = (hn * jax.nn.sigmoid(og.astype(F32))).astype(hm_ref.dtype)
        b_last = b_col[L - 1:L, :]
        g_last = b_last + m_st
        wlog = b_last - b_col + ig_col
        m_new = jnp.maximum(g_last, jnp.max(wlog, axis=0, keepdims=True))
        w_s = jnp.exp(wlog - m_new) * scale
        decay = jnp.exp(g_last - m_new)
        kv = lax.dot_general(k, (w_s * v.astype(F32)).astype(BF16), tn, preferred_element_type=F32)
        c_ref[h] = decay * c_st + kv
        n_ref[h:h + 1, :] = decay * n_st + jnp.sum(w_s * k.astype(F32), axis=0, keepdims=True)
        m_tile = jnp.where(lane1 == h, m_new, m_tile)
    m_ref[...] = m_tile


def _mlstm_prompt(u_main, gates, gate_bias, g_mlstm, batch, seq):
    m = batch * seq
    L = M_CHUNK
    nc = seq // L
    half = 3 * A_W
    blk0 = (9 * A_W) // half
    return pl.pallas_call(
        _mlstm_kernel,
        grid=(batch, nc),
        in_specs=[
            pl.BlockSpec((L, half), lambda b, c: (b * nc + c, blk0)),
            pl.BlockSpec((L, half), lambda b, c: (b * nc + c, blk0 + 1)),
            pl.BlockSpec((L, LANES), lambda b, c: (b * nc + c, 0)),
            pl.BlockSpec((1, LANES), lambda b, c: (0, 0)),
            pl.BlockSpec((1, M_V), lambda b, c: (0, 0)),
        ],
        out_specs=[
            pl.BlockSpec((L, M_V), lambda b, c: (b * nc + c, 0)),
            pl.BlockSpec((None, N_HEADS, HD, DV), lambda b, c: (b, 0, 0, 0)),
            pl.BlockSpec((None, N_HEADS, HD), lambda b, c: (b, 0, 0)),
            pl.BlockSpec((None, 1, LANES), lambda b, c: (b, 0, 0)),
        ],
        out_shape=[
            jax.ShapeDtypeStruct((m, M_V), BF16),
            jax.ShapeDtypeStruct((batch, N_HEADS, HD, DV), F32),
            jax.ShapeDtypeStruct((batch, N_HEADS, HD), F32),
            jax.ShapeDtypeStruct((batch, 1, LANES), F32),
        ],
        compiler_params=_cparams(("parallel", "arbitrary")),
        name="mlstm_prompt",
    )(u_main, u_main, gates, gate_bias, g_mlstm.reshape(1, M_V))


def _mid_sample_kernel(u_ref, r_ref, k1_ref, k2_ref, k3_ref, st_ref, cw_ref, a_ref, co_ref, ns_ref):
    scale = HD ** -0.5
    caches = (k1_ref, k2_ref, k3_ref)
    o_heads = [[None] * N_HEADS for _ in range(N_GROUPS)]
    lse_heads = [[None] * N_HEADS for _ in range(N_GROUPS)]
    for g in range(N_GROUPS):
        base = 3 * g * A_W
        for h in range(N_HEADS):
            q = u_ref[:, base + h * HD:base + (h + 1) * HD].astype(F32)
            k_new = u_ref[:, base + A_W + h * HD:base + A_W + (h + 1) * HD].astype(F32)
            v_new = u_ref[:, base + 2 * A_W + h * HD:base + 2 * A_W + (h + 1) * HD].astype(F32)
            k_c = caches[g][:, h * HD:(h + 1) * HD]
            v_c = caches[g][:, A_W + h * HD:A_W + (h + 1) * HD]
            s_c = jnp.sum(k_c * q, axis=-1, keepdims=True) * scale
            s_n = jnp.sum(k_new * q, axis=-1, keepdims=True) * scale
            mx = jnp.maximum(jnp.max(s_c, axis=0, keepdims=True), s_n)
            p_c = jnp.exp(s_c - mx)
            p_n = jnp.exp(s_n - mx)
            den = jnp.sum(p_c, axis=0, keepdims=True) + p_n
            o = (jnp.sum(p_c * v_c, axis=0, keepdims=True) + p_n * v_new) / den
            o_heads[g][h] = o
            lse_heads[g][h] = mx + jnp.log(den)
    for h in range(N_HEADS):
        l0, l1, l2 = lse_heads[0][h], lse_heads[1][h], lse_heads[2][h]
        mx = jnp.maximum(jnp.maximum(l0, l1), l2)
        e0, e1, e2 = jnp.exp(l0 - mx), jnp.exp(l1 - mx), jnp.exp(l2 - mx)
        den = e0 + e1 + e2
        a = (e0 / den) * o_heads[0][h] + (e1 / den) * o_heads[1][h] + (e2 / den) * o_heads[2][h]
        a_ref[:, h * HD:(h + 1) * HD] = a.astype(a_ref.dtype)
    p = r_ref[:, 2 * W_C:3 * W_C].astype(F32) * r_ref[:, 0:W_C].astype(F32)
    z = cw_ref[0:1, :] * st_ref[0:1, :] + cw_ref[1:2, :] * st_ref[1:2, :] + cw_ref[2:3, :] * p
    co_ref[...] = (r_ref[:, W_C:2 * W_C].astype(F32) * z).astype(co_ref.dtype)
    ns_ref[0:1, :] = st_ref[1:2, :]
    ns_ref[1:2, :] = p


def _mid_sample(u_main, u_rest, caches, state_conv, conv_w, layer):
    nb = u_main.shape[0]
    views = []
    cache_specs = []
    for g in range(N_GROUPS):
        dil = A_DILATIONS[g]
        c = caches[g]
        assert c.shape[2] == A_WINDOWS[g]
        views.append(c.reshape(c.shape[0], nb, A_BLK, dil * 2 * A_W))
        cache_specs.append(pl.BlockSpec((None, None, A_BLK, 2 * A_W), lambda b: (layer, b, 0, 0)))
    return pl.pallas_call(
        _mid_sample_kernel,
        grid=(nb,),
        in_specs=[
            pl.BlockSpec((None, 1, U_MAIN), lambda b: (b, 0, 0)),
            pl.BlockSpec((None, 1, 3 * W_C), lambda b: (b, 0, 0)),
        ] + cache_specs + [
            pl.BlockSpec((None, None, CONV_W - 1, W_C), lambda b: (layer, b, 0, 0)),
            pl.BlockSpec((CONV_W, W_C), lambda b: (0, 0)),
        ],
        out_specs=[
            pl.BlockSpec((None, 1, A_W), lambda b: (b, 0, 0)),
            pl.BlockSpec((None, 1, W_C), lambda b: (b, 0, 0)),
            pl.BlockSpec((None, CONV_W - 1, W_C), lambda b: (b, 0, 0)),
        ],
        out_shape=[
            jax.ShapeDtypeStruct((nb, 1, A_W), BF16),
            jax.ShapeDtypeStruct((nb, 1, W_C), BF16),
            jax.ShapeDtypeStruct((nb, CONV_W - 1, W_C), F32),
        ],
        compiler_params=_cparams(("parallel",)),
        name="mid_sample",
    )(u_main.reshape(nb, 1, U_MAIN), u_rest.reshape(nb, 1, -1), *views, state_conv, conv_w)


def _mlstm_step_kernel(u_ref, gt_ref, bias_ref, gm_ref, c_ref, n_ref, m_ref,
                       hm_ref, co_ref, no_ref, mo_ref):
    scale = HD ** -0.5
    g = gt_ref[...] + bias_ref[...]
    lf_all = _log_sigmoid(g)
    m_all = m_ref[...]
    lane1 = lax.broadcasted_iota(jnp.int32, (1, LANES), 1)
    eye = (lax.broadcasted_iota(jnp.int32, (HD, HD), 0)
           == lax.broadcasted_iota(jnp.int32, (HD, HD), 1))
    m_tile = jnp.zeros((1, LANES), F32)
    off = 9 * A_W
    for h in range(N_HEADS):
        ig = g[:, h:h + 1]
        lf = lf_all[:, N_HEADS + h:N_HEADS + h + 1]
        m_st = m_all[:, h:h + 1]
        q = u_ref[:, off + h * HD:off + (h + 1) * HD]
        k = u_ref[:, off + M_QK + h * HD:off + M_QK + (h + 1) * HD]
        v = u_ref[:, off + 2 * M_QK + h * DV:off + 2 * M_QK + (h + 1) * DV]
        og = u_ref[:, off + 2 * M_QK + M_V + h * DV:off + 2 * M_QK + M_V + (h + 1) * DV]
        qf, kf, vf = q.astype(F32), k.astype(F32), v.astype(F32)
        c_st = c_ref[h]
        n_st = n_ref[h:h + 1, :]
        g_col = lf + m_st
        m_row = jnp.maximum(g_col, ig)
        w_intra = jnp.exp(ig - m_row)
        w_inter = jnp.exp(g_col - m_row)
        s = jnp.sum(qf * kf, axis=-1, keepdims=True) * (w_intra * scale)
        q_rows = jnp.broadcast_to(qf, (16, HD)).astype(BF16)
        qc = jnp.dot(q_rows, c_st.astype(BF16), preferred_element_type=F32)[0:1, :]
        num = s * vf + w_inter * qc
        den = s + w_inter * jnp.sum(qf * n_st, axis=-1, keepdims=True)
        hh = num / jnp.maximum(jnp.abs(den), jnp.exp(-m_row))
        hn = hh * lax.rsqrt(jnp.mean(hh * hh, axis=-1, keepdims=True) + EPS) * gm_ref[:, h * DV:(h + 1) * DV]
        hm_ref[:, h * DV:(h + 1) * DV] = (hn * jax.nn.sigmoid(og.astype(F32))).astype(hm_ref.dtype)
        w_s = w_intra * scale
        k_diag = jnp.where(eye, jnp.broadcast_to(kf, (HD, HD)), 0.0).astype(BF16)
        v_rows = jnp.broadcast_to(w_s * vf, (HD, DV)).astype(BF16)
        kv = jnp.dot(k_diag, v_rows, preferred_element_type=F32)
        co_ref[h] = w_inter * c_st + kv
        no_ref[h:h + 1, :] = w_inter * n_st + w_s * kf
        m_tile = jnp.where(lane1 == h, m_row, m_tile)
    mo_ref[...] = m_tile


def _mlstm_step(u_main, gates, gate_bias, g_mlstm, state_c, state_n, m_pad, layer):
    nb = u_main.shape[0]
    return pl.pallas_call(
        _mlstm_step_kernel,
        grid=(nb,),
        in_specs=[
            pl.BlockSpec((None, 1, U_MAIN), lambda b: (b, 0, 0)),
            pl.BlockSpec((None, 1, LANES), lambda b: (b, 0, 0)),
            pl.BlockSpec((1, LANES), lambda b: (0, 0)),
            pl.BlockSpec((1, M_V), lambda b: (0, 0)),
            pl.BlockSpec((None, None, N_HEADS, HD, DV), lambda b: (layer, b, 0, 0, 0)),
            pl.BlockSpec((None, None, N_HEADS, HD), lambda b: (layer, b, 0, 0)),
            pl.BlockSpec((None, 1, LANES), lambda b: (b, 0, 0)),
        ],
        out_specs=[
            pl.BlockSpec((None, 1, M_V), lambda b: (b, 0, 0)),
            pl.BlockSpec((None, N_HEADS, HD, DV), lambda b: (b, 0, 0, 0)),
            pl.BlockSpec((None, N_HEADS, HD), lambda b: (b, 0, 0)),
            pl.BlockSpec((None, 1, LANES), lambda b: (b, 0, 0)),
        ],
        out_shape=[
            jax.ShapeDtypeStruct((nb, 1, M_V), BF16),
            jax.ShapeDtypeStruct((nb, N_HEADS, HD, DV), F32),
            jax.ShapeDtypeStruct((nb, N_HEADS, HD), F32),
            jax.ShapeDtypeStruct((nb, 1, LANES), F32),
        ],
        compiler_params=_cparams(("parallel",)),
        name="mlstm_step",
    )(u_main.reshape(nb, 1, U_MAIN), gates.reshape(nb, 1, LANES), gate_bias,
      g_mlstm.reshape(1, M_V), state_c, state_n, m_pad)


def _pad_lanes(t):
    return jnp.pad(t, [(0, 0)] * (t.ndim - 1) + [(0, LANES - t.shape[-1])])


def _run_group(x, mods, states, p, prompt):
    batch, seq, d = x.shape
    m = batch * seq
    depth = p["w_in"].shape[0]
    tm_big = 1024 if prompt else m
    tm_mid = 512 if prompt else m
    xf = x.reshape(m, d)
    new_states = []
    for l in range(depth):
        sh1, sc1, gt1, sh2, sc2, gt2 = mods[l]
        u_main, gates, u_rest = _in_proj(xf, p["g_norm1"][l], sh1, sc1, p["w_in"], l,
                                         p["w_gate_pad"][l], p["w_rest"][l], tm_big)
        gate_bias = p["gate_bias"][l]
        if prompt:
            outs = [_attn_prompt(u_main, batch, seq, g) for g in range(N_GROUPS)]
            a_out, c_out, new_conv = _mid_prompt([o for o, _ in outs], [s for _, s in outs],
                                                 u_rest, p["conv_w"][l], batch, seq)
            hm, c_n, n_n, m_n = _mlstm_prompt(u_main, gates, gate_bias, p["g_mlstm"][l], batch, seq)
            bufs = []
            for g in range(N_GROUPS):
                keep = min(A_WINDOWS[g], seq)
                u3 = u_main.reshape(batch, seq, U_MAIN)[:, seq - keep:]
                k = u3[..., (3 * g + 1) * A_W:(3 * g + 2) * A_W]
                v = u3[..., (3 * g + 2) * A_W:(3 * g + 3) * A_W]
                kv = jnp.stack([k, v], axis=2).astype(F32)
                bufs.append(kv.reshape(batch, keep, 2, N_HEADS, HD))
        else:
            caches, st_c, st_n, st_m, st_conv = states
            a_out, c_out, new_conv = _mid_sample(u_main, u_rest[:, :3 * W_C], caches, st_conv,
                                                 p["conv_w"][l], l)
            a_out = a_out.reshape(m, A_W)
            c_out = c_out.reshape(m, W_C)
            hm, c_n, n_n, m_n = _mlstm_step(u_main, gates, gate_bias, p["g_mlstm"][l],
                                            st_c, st_n, _pad_lanes(st_m[l])[:, None, :], l)
            hm = hm.reshape(m, M_V)
            bufs = []
            for g in range(N_GROUPS):
                k = u_main[:, (3 * g + 1) * A_W:(3 * g + 2) * A_W]
                v = u_main[:, (3 * g + 2) * A_W:(3 * g + 3) * A_W]
                kv = jnp.stack([k, v], axis=1).astype(F32).reshape(batch, seq, 2, N_HEADS, HD)
                bufs.append(jnp.concatenate([caches[g][l][:, seq:], kv], axis=1))
        merged = _branch_merge(a_out, hm, c_out, u_rest, p["w_br_a"], p["w_br_m"], p["w_br_c"], l, tm_mid)
        xf = _mm_res(merged, p["w_mix_out"], l, xf, gt1, None, tm_mid)
        act = _ffn_up(xf, p["g_norm2"][l], sh2, sc2, p["w_ffn_gate"], p["w_ffn_up"], l, tm_big)
        xf = _mm_res(act, p["w_ffn_down"], l, xf, gt2,
                     p["g_final"] if l == depth - 1 else None, tm_mid)
        new_states.append(bufs + [c_n, n_n, m_n[:, 0, :N_HEADS], new_conv])
    stacked = [jnp.stack([n[i] for n in new_states], axis=0) for i in range(7)]
    return xf.reshape(batch, seq, d), stacked


def kernel(x_prompt, x_sample, cache_a1_kv, cache_a2_kv, cache_a3_kv, state_mlstm_c, state_mlstm_n,
           state_mlstm_m, state_conv, c_prompt, c_sample, w_ada, b_ada, g_norm1, g_norm2, w_in, b_igate,
           b_fgate, g_mlstm, conv_w, w_br_a, w_br_m, w_br_c, w_mix_out, w_ffn_gate, w_ffn_up, w_ffn_down,
           g_final):
    depth, d = g_norm1.shape
    nbp = c_prompt.shape[0]
    nbs = c_sample.shape[0]
    assert w_in.shape[-1] == U_REST_OFF + 3 * W_C + 3 * d

    c_all = jnp.concatenate([c_prompt, c_sample], axis=0)
    c_all = jnp.pad(c_all, ((0, (-c_all.shape[0]) % 8), (0, 0)))
    ada = _ada(c_all, w_ada, b_ada)
    mods_p, mods_s = [], []
    for l in range(depth):
        mods_p.append([ada[l, :nbp, i * d:(i + 1) * d].reshape(nbp, 1, d) for i in range(6)])
        mods_s.append([ada[l, nbp:nbp + nbs, i * d:(i + 1) * d].reshape(1, nbs, d) for i in range(6)])

    params = dict(
        w_in=w_in, g_norm1=g_norm1, g_norm2=g_norm2, g_mlstm=g_mlstm, conv_w=conv_w,
        w_br_a=w_br_a, w_br_m=w_br_m, w_br_c=w_br_c, w_mix_out=w_mix_out,
        w_ffn_gate=w_ffn_gate, w_ffn_up=w_ffn_up, w_ffn_down=w_ffn_down, g_final=g_final,
        w_rest=w_in[:, :, U_REST_OFF:].astype(BF16),
        w_gate_pad=_pad_lanes(w_in[:, :, U_MAIN:U_REST_OFF]),
        gate_bias=_pad_lanes(jnp.concatenate([b_igate, b_fgate], axis=-1))[:, None, :],
    )

    y_p, ps = _run_group(x_prompt, mods_p, None, params, True)
    states = ((cache_a1_kv, cache_a2_kv, cache_a3_kv), state_mlstm_c, state_mlstm_n, state_mlstm_m,
              state_conv)
    y_s, ss = _run_group(x_sample, mods_s, states, params, False)
    return (y_p, y_s, ps[0], ss[0], ps[1], ss[1], ps[2], ss[2], ps[3], ss[3], ps[4], ss[4],
            ps[5], ss[5], ps[6], ss[6])
```

```python
import functools

import jax
import jax.numpy as jnp
from jax import lax
from jax.experimental import pallas as pl
from jax.experimental.pallas import tpu as pltpu

F32 = jnp.float32
BF16 = jnp.bfloat16
EPS = 1e-6

A_WINDOWS = (128, 512, 2048)
A_DILATIONS = (1, 4, 16)
N_GROUPS = 3
HD = 128
A_BLK = 128
N_HEADS = 4
DV = 256
A_W = N_HEADS * HD
M_QK = N_HEADS * HD
M_V = N_HEADS * DV
W_C = 1024
CONV_W = 3
U_ATTN = 9 * A_W
U_MLSTM = 2 * M_QK + 2 * M_V
U_MAIN = U_ATTN + U_MLSTM
N_GATE_COLS = 2 * N_HEADS
U_REST_OFF = U_MAIN + N_GATE_COLS
LANES = 128
SUBLANES = 8
SEG = A_DILATIONS[-1] * A_BLK

VMEM_LIMIT = 52 * 1024 * 1024
NT_DIMS = (((1,), (1,)), ((), ()))


def _cparams(sem):
    return pltpu.CompilerParams(dimension_semantics=sem, vmem_limit_bytes=VMEM_LIMIT)


def _ada_kernel(c_ref, w_ref, b_ref, o_ref):
    c = c_ref[...]
    a = (c * jax.nn.sigmoid(c)).astype(BF16)
    o_ref[...] = jnp.dot(a, w_ref[...].astype(BF16), preferred_element_type=F32) + b_ref[...]


def _ada(c_all, w_ada, b_ada, tn=1024):
    depth, d, n = w_ada.shape
    r = c_all.shape[0]
    return pl.pallas_call(
        _ada_kernel,
        grid=(depth, n // tn),
        in_specs=[
            pl.BlockSpec((r, d), lambda l, j: (0, 0)),
            pl.BlockSpec((None, d, tn), lambda l, j: (l, 0, j)),
            pl.BlockSpec((None, 1, tn), lambda l, j: (l, 0, j)),
        ],
        out_specs=pl.BlockSpec((None, r, tn), lambda l, j: (l, 0, j)),
        out_shape=jax.ShapeDtypeStruct((depth, r, n), F32),
        compiler_params=_cparams(("parallel", "parallel")),
        name="ada",
    )(c_all, w_ada, b_ada.reshape(depth, 1, n))


def _modulated_norm(x_ref, g_ref, sh_ref, sc_ref, h_scr):
    x = x_ref[...]
    ms = jnp.mean(x * x, axis=-1, keepdims=True)
    y = x * lax.rsqrt(ms + EPS) * g_ref[...]
    h_scr[...] = (y * (1.0 + sc_ref[...]) + sh_ref[...]).astype(BF16)


def _row_specs(m, d, tm, shift):
    groups = shift.shape[0]
    blocks_per_group = (m // groups) // tm
    mod = pl.BlockSpec((None, shift.shape[1], d), lambda i, j: (i // blocks_per_group, 0, 0))
    return [
        pl.BlockSpec((tm, d), lambda i, j: (i, 0)),
        pl.BlockSpec((1, d), lambda i, j: (0, 0)),
        mod,
        mod,
    ]


IN_TN = 512
ATTN_TILES = U_ATTN // IN_TN
MAIN_TILES = U_MAIN // IN_TN


def _in_proj_kernel(x_ref, g_ref, sh_ref, sc_ref, w_ref, wg_ref, wn_ref,
                    oa_ref, om_ref, og_ref, or_ref, h_scr):
    j = pl.program_id(1)

    @pl.when(j == 0)
    def _():
        _modulated_norm(x_ref, g_ref, sh_ref, sc_ref, h_scr)
        wg = jnp.concatenate([wg_ref[...], jnp.zeros((LANES - N_GATE_COLS, wg_ref.shape[1]), F32)], axis=0)
        og_ref[...] = lax.dot_general(h_scr[...], wg.astype(BF16), NT_DIMS, preferred_element_type=F32)

    @pl.when(j < ATTN_TILES)
    def _():
        oa_ref[...] = lax.dot_general(h_scr[...], w_ref[...].astype(BF16), NT_DIMS,
                                      preferred_element_type=F32)

    @pl.when(jnp.logical_and(j >= ATTN_TILES, j < MAIN_TILES))
    def _():
        om_ref[...] = lax.dot_general(h_scr[...], w_ref[...].astype(BF16), NT_DIMS,
                                      preferred_element_type=F32).astype(om_ref.dtype)

    @pl.when(j >= MAIN_TILES)
    def _():
        w = jnp.concatenate([w_ref[N_GATE_COLS:, :], wn_ref[...]], axis=0).astype(BF16)
        or_ref[...] = lax.dot_general(h_scr[...], w, NT_DIMS,
                                      preferred_element_type=F32).astype(or_ref.dtype)


def _in_proj(x, g_norm, shift, scale, w_t, layer, tm):
    m, d = x.shape
    tn = IN_TN
    n_rest = w_t.shape[1] - U_REST_OFF
    n_tiles = MAIN_TILES + n_rest // tn
    assert N_GATE_COLS == SUBLANES and n_rest % tn == 0
    rows_per_tile = tn // SUBLANES
    return pl.pallas_call(
        _in_proj_kernel,
        grid=(m // tm, n_tiles),
        in_specs=_row_specs(m, d, tm, shift) + [
            pl.BlockSpec((None, tn, d), lambda i, j: (layer, j, 0)),
            pl.BlockSpec((None, SUBLANES, d), lambda i, j: (layer, U_MAIN // SUBLANES, 0)),
            pl.BlockSpec((None, SUBLANES, d),
                         lambda i, j: (layer, (jnp.maximum(j, MAIN_TILES) + 1) * rows_per_tile, 0)),
        ],
        out_specs=[
            pl.BlockSpec((tm, tn), lambda i, j: (i, jnp.minimum(j, ATTN_TILES - 1))),
            pl.BlockSpec((tm, tn), lambda i, j: (i, jnp.clip(j - ATTN_TILES, 0, MAIN_TILES - ATTN_TILES - 1))),
            pl.BlockSpec((tm, LANES), lambda i, j: (i, 0)),
            pl.BlockSpec((tm, tn), lambda i, j: (i, jnp.maximum(j - MAIN_TILES, 0))),
        ],
        out_shape=[
            jax.ShapeDtypeStruct((m, U_ATTN), F32),
            jax.ShapeDtypeStruct((m, U_MLSTM), BF16),
            jax.ShapeDtypeStruct((m, LANES), F32),
            jax.ShapeDtypeStruct((m, n_rest), BF16),
        ],
        scratch_shapes=[pltpu.VMEM((tm, d), BF16)],
        compiler_params=_cparams(("parallel", "arbitrary")),
        name="in_proj",
    )(x, g_norm.reshape(1, d), shift, scale, w_t, w_t, w_t)


def _ffn_up_kernel(x_ref, g_ref, sh_ref, sc_ref, wg_ref, wu_ref, o_ref, h_scr):
    @pl.when(pl.program_id(1) == 0)
    def _():
        _modulated_norm(x_ref, g_ref, sh_ref, sc_ref, h_scr)

    h = h_scr[...]
    a = jnp.dot(h, wg_ref[...].astype(BF16), preferred_element_type=F32)
    b = jnp.dot(h, wu_ref[...].astype(BF16), preferred_element_type=F32)
    o_ref[...] = (a * jax.nn.sigmoid(a) * b).astype(o_ref.dtype)


def _ffn_up(x, g_norm, shift, scale, w_gate, w_up, layer, tm, tn=512):
    m, d = x.shape
    d_ff = w_gate.shape[-1]
    wspec = pl.BlockSpec((None, d, tn), lambda i, j: (layer, 0, j))
    return pl.pallas_call(
        _ffn_up_kernel,
        grid=(m // tm, d_ff // tn),
        in_specs=_row_specs(m, d, tm, shift) + [wspec, wspec],
        out_specs=pl.BlockSpec((tm, tn), lambda i, j: (i, j)),
        out_shape=jax.ShapeDtypeStruct((m, d_ff), BF16),
        scratch_shapes=[pltpu.VMEM((tm, d), BF16)],
        compiler_params=_cparams(("parallel", "arbitrary")),
        name="ffn_up",
    )(x, g_norm.reshape(1, d), shift, scale, w_gate, w_up)


def _mm_res_kernel(final, a_ref, w_ref, res_ref, gt_ref, *rest):
    if final:
        gf_ref, o_ref = rest
    else:
        (o_ref,) = rest
    k = pl.program_id(1)
    tm, d = o_ref.shape
    half = d // 2
    for c in range(2):
        sl = slice(c * half, (c + 1) * half)
        part = jnp.dot(a_ref[...], w_ref[:, sl].astype(BF16), preferred_element_type=F32)

        @pl.when(k == 0)
        def _():
            o_ref[:, sl] = part

        @pl.when(k > 0)
        def _():
            o_ref[:, sl] += part

    @pl.when(k == pl.num_programs(1) - 1)
    def _():
        rows = min(tm, 128)

        def finish(r, carry):
            rs = pl.ds(pl.multiple_of(r * rows, rows), rows)
            x = res_ref[rs, :] + gt_ref[...] * o_ref[rs, :]
            if final:
                ms = jnp.mean(x * x, axis=-1, keepdims=True)
                x = x * lax.rsqrt(ms + EPS) * gf_ref[...]
            o_ref[rs, :] = x
            return carry
        lax.fori_loop(0, tm // rows, finish, 0)


def _mm_res(a, w, layer, res, gate, g_final, tm, tk=512):
    m, kdim = a.shape
    d = res.shape[-1]
    groups = gate.shape[0]
    blocks_per_group = (m // groups) // tm
    final = g_final is not None
    in_specs = [
        pl.BlockSpec((tm, tk), lambda i, k: (i, k)),
        pl.BlockSpec((None, tk, d), lambda i, k: (layer, k, 0)),
        pl.BlockSpec((tm, d), lambda i, k: (i, 0), pipeline_mode=pl.Buffered(1)),
        pl.BlockSpec((None, gate.shape[1], d), lambda i, k: (i // blocks_per_group, 0, 0)),
    ]
    args = [a, w, res, gate]
    if final:
        in_specs.append(pl.BlockSpec((1, d), lambda i, k: (0, 0)))
        args.append(g_final.reshape(1, d))
    return pl.pallas_call(
        functools.partial(_mm_res_kernel, final),
        grid=(m // tm, kdim // tk),
        in_specs=in_specs,
        out_specs=pl.BlockSpec((tm, d), lambda i, k: (i, 0)),
        out_shape=jax.ShapeDtypeStruct((m, d), F32),
        compiler_params=_cparams(("parallel", "arbitrary")),
        name="mm_res_final" if final else "mm_res",
    )(*args)


def _branch_kernel(a_ref, m_ref, c_ref, ga_ref, gm_ref, gc_ref, wa_ref, wm_ref, wc_ref, o_ref):
    def term(x_ref, w_ref, g_ref):
        y = jnp.dot(x_ref[...], w_ref[...].astype(BF16), preferred_element_type=F32)
        return jax.nn.sigmoid(g_ref[...].astype(F32)) * y

    o_ref[...] = (term(a_ref, wa_ref, ga_ref) + term(m_ref, wm_ref, gm_ref)
                  + term(c_ref, wc_ref, gc_ref)).astype(o_ref.dtype)


def _branch_merge(a_out, hm, c_out, u_rest, w_br_a, w_br_m, w_br_c, layer, tm, tn=512):
    m = a_out.shape[0]
    d = w_br_a.shape[-1]
    gate0 = (3 * W_C) // tn
    per_gate = d // tn

    def gate_spec(g):
        return pl.BlockSpec((tm, tn), lambda i, j: (i, gate0 + g * per_gate + j))

    def w_spec(k):
        return pl.BlockSpec((None, k, tn), lambda i, j: (layer, 0, j))

    return pl.pallas_call(
        _branch_kernel,
        grid=(m // tm, d // tn),
        in_specs=[
            pl.BlockSpec((tm, A_W), lambda i, j: (i, 0)),
            pl.BlockSpec((tm, M_V), lambda i, j: (i, 0)),
            pl.BlockSpec((tm, W_C), lambda i, j: (i, 0)),
            gate_spec(0), gate_spec(1), gate_spec(2),
            w_spec(A_W), w_spec(M_V), w_spec(W_C),
        ],
        out_specs=pl.BlockSpec((tm, tn), lambda i, j: (i, j)),
        out_shape=jax.ShapeDtypeStruct((m, d), BF16),
        compiler_params=_cparams(("parallel", "parallel")),
        name="branch_merge",
    )(a_out, hm, c_out, u_rest, u_rest, u_rest, w_br_a, w_br_m, w_br_c)


ATTN_NB = 4


def _attn_blocks(blocks):
    qi = lax.broadcasted_iota(jnp.int32, (A_BLK, A_BLK), 0)
    ki = lax.broadcasted_iota(jnp.int32, (A_BLK, A_BLK), 1)
    dist = ki - qi
    scale = HD ** -0.5
    n = len(blocks)
    qb = [b[0].astype(BF16) for b in blocks]
    s_cur = [lax.dot_general(qb[i], blocks[i][1].astype(BF16), NT_DIMS, preferred_element_type=F32)
             for i in range(n)]
    s_prev = [lax.dot_general(qb[i], blocks[i][2].astype(BF16), NT_DIMS, preferred_element_type=F32)
              for i in range(n)]
    s_cur = [jnp.where(dist <= 0, s * scale, -jnp.inf) for s in s_cur]
    s_prev = [jnp.where(dist >= blocks[i][5], s_prev[i] * scale, -jnp.inf) for i in range(n)]
    mx = [jnp.maximum(jnp.max(s_cur[i], axis=-1, keepdims=True), jnp.max(s_prev[i], axis=-1, keepdims=True))
          for i in range(n)]
    p_cur = [jnp.exp(s_cur[i] - mx[i]) for i in range(n)]
    p_prev = [jnp.exp(s_prev[i] - mx[i]) for i in range(n)]
    den = [jnp.sum(p_cur[i], axis=-1, keepdims=True) + jnp.sum(p_prev[i], axis=-1, keepdims=True)
           for i in range(n)]
    o = [jnp.dot(p_cur[i].astype(BF16), blocks[i][3].astype(BF16), preferred_element_type=F32)
         + jnp.dot(p_prev[i].astype(BF16), blocks[i][4].astype(BF16), preferred_element_type=F32)
         for i in range(n)]
    return [(o[i] / den[i], mx[i] + jnp.log(den[i])) for i in range(n)]


def _attn_kernel(*refs):
    ins = refs[:5 * N_GROUPS]
    a_ref = refs[5 * N_GROUPS]
    o_scr = refs[5 * N_GROUPS + 1:5 * N_GROUPS + 1 + N_GROUPS]
    l_scr = refs[5 * N_GROUPS + 1 + N_GROUPS:]
    nb = ATTN_NB
    first_lim = jnp.where(pl.program_id(1) > 0, 0, A_BLK)
    for g in range(N_GROUPS):
        d = A_DILATIONS[g]
        q_ref, kc_ref, kp_ref, vc_ref, vp_ref = ins[5 * g:5 * g + 5]
        span = d * A_BLK
        nblk = SEG // span

        def rows(start, d=d):
            return pl.ds(start, A_BLK, stride=d) if d > 1 else pl.ds(start, A_BLK)

        def head(r, rows=rows, q_ref=q_ref, kc_ref=kc_ref, kp_ref=kp_ref, vc_ref=vc_ref, vp_ref=vp_ref):
            return (q_ref[rows(r), :], kc_ref[rows(r), :], kp_ref[rows(r), :],
                    vc_ref[rows(r), :], vp_ref[rows(r), :], first_lim)

        def inner(start, rows=rows, span=span, q_ref=q_ref, kc_ref=kc_ref, vc_ref=vc_ref):
            return (q_ref[rows(start), :], kc_ref[rows(start), :], kc_ref[rows(start - span), :],
                    vc_ref[rows(start), :], vc_ref[rows(start - span), :], 0)

        def run(starts, blocks, g=g, rows=rows):
            for start, (o, lse) in zip(starts, _attn_blocks(blocks)):
                o_scr[g][rows(start), :] = o
                l_scr[g][rows(start), :] = jnp.broadcast_to(lse, (A_BLK, LANES))

        if nblk == 1:
            def body(t, c, run=run, head=head):
                rs = [t * nb + i for i in range(nb)]
                run(rs, [head(r) for r in rs])
                return c
            lax.fori_loop(0, d // nb, body, 0)
        else:
            assert nblk % nb == 0

            def body(r, c, run=run, head=head, inner=inner, span=span, nblk=nblk):
                starts = [r + i * span for i in range(nb)]
                run(starts, [head(r)] + [inner(s) for s in starts[1:]])

                def more(t, c2):
                    st = [r + (t * nb + i) * span for i in range(nb)]
                    run(st, [inner(s) for s in st])
                    return c2
                if nblk > nb:
                    lax.fori_loop(1, nblk // nb, more, 0)
                return c
            if d == 1:
                body(0, 0)
            else:
                lax.fori_loop(0, d, body, 0)

    chunk = 64

    def merge(c, carry):
        sl = pl.ds(pl.multiple_of(c * chunk, chunk), chunk)
        l0, l1, l2 = l_scr[0][sl, :], l_scr[1][sl, :], l_scr[2][sl, :]
        mx = jnp.maximum(jnp.maximum(l0, l1), l2)
        e0, e1, e2 = jnp.exp(l0 - mx), jnp.exp(l1 - mx), jnp.exp(l2 - mx)
        den = e0 + e1 + e2
        a = (e0 / den) * o_scr[0][sl, :] + (e1 / den) * o_scr[1][sl, :] + (e2 / den) * o_scr[2][sl, :]
        a_ref[sl, :] = a.astype(a_ref.dtype)
        return carry
    lax.fori_loop(0, SEG // chunk, merge, 0)


def _attn_prompt(u_attn, batch, seq):
    assert seq % SEG == 0
    nseg = seq // SEG
    specs = []
    for g in range(N_GROUPS):
        d = A_DILATIONS[g]
        assert A_WINDOWS[g] // d == A_BLK
        per_seg = SEG // (d * A_BLK)

        def cur(part, g=g):
            return pl.BlockSpec((SEG, HD), lambda b, s, h: (b * nseg + s, (3 * g + part) * N_HEADS + h))

        def prev(part, g=g, d=d, per_seg=per_seg):
            return pl.BlockSpec((d * A_BLK, HD),
                                lambda b, s, h: (jnp.maximum((b * nseg + s) * per_seg - 1, 0),
                                                 (3 * g + part) * N_HEADS + h))
        specs += [cur(0), cur(1), prev(1), cur(2), prev(2)]
    return pl.pallas_call(
        _attn_kernel,
        grid=(batch, nseg, N_HEADS),
        in_specs=specs,
        out_specs=pl.BlockSpec((SEG, HD), lambda b, s, h: (b * nseg + s, h)),
        out_shape=jax.ShapeDtypeStruct((batch * seq, A_W), BF16),
        scratch_shapes=[pltpu.VMEM((SEG, HD), F32)] * (2 * N_GROUPS),
        compiler_params=_cparams(("parallel", "parallel", "parallel")),
        name="attn_prompt",
    )(*([u_attn] * (5 * N_GROUPS)))


def _conv_prompt_kernel(blocks_per_seq, u_ref, b_ref, c_ref, cw_ref, co_ref, st_ref, p_scr):
    tm = u_ref.shape[0]
    i = pl.program_id(0)

    @pl.when(i % blocks_per_seq == 0)
    def _():
        p_scr[0:8, :] = jnp.zeros((8, p_scr.shape[1]), F32)

    @pl.when(i % blocks_per_seq != 0)
    def _():
        p_scr[0:8, :] = p_scr[tm:tm + 8, :]

    p = c_ref[...].astype(F32) * u_ref[...].astype(F32)
    p_scr[8:8 + tm, :] = p
    z = (cw_ref[0:1, :] * p_scr[6:6 + tm, :] + cw_ref[1:2, :] * p_scr[7:7 + tm, :]
         + cw_ref[2:3, :] * p)
    co_ref[...] = (b_ref[...].astype(F32) * z).astype(co_ref.dtype)
    st_ref[...] = p[tm - (CONV_W - 1):tm, :]


def _conv_prompt(u_rest, conv_w, batch, seq, tm=512):
    m = batch * seq
    blocks_per_seq = seq // tm
    return pl.pallas_call(
        functools.partial(_conv_prompt_kernel, blocks_per_seq),
        grid=(m // tm,),
        in_specs=[
            pl.BlockSpec((tm, W_C), lambda i: (i, 0)),
            pl.BlockSpec((tm, W_C), lambda i: (i, 1)),
            pl.BlockSpec((tm, W_C), lambda i: (i, 2)),
            pl.BlockSpec((CONV_W, W_C), lambda i: (0, 0)),
        ],
        out_specs=[
            pl.BlockSpec((tm, W_C), lambda i: (i, 0)),
            pl.BlockSpec((None, CONV_W - 1, W_C), lambda i: (i // blocks_per_seq, 0, 0)),
        ],
        out_shape=[
            jax.ShapeDtypeStruct((m, W_C), BF16),
            jax.ShapeDtypeStruct((batch, CONV_W - 1, W_C), F32),
        ],
        scratch_shapes=[pltpu.VMEM((tm + 8, W_C), F32)],
        compiler_params=_cparams(("arbitrary",)),
        name="conv_prompt",
    )(u_rest, u_rest, u_rest, conv_w)


M_CHUNK = 128


def _log_sigmoid(x):
    return jnp.minimum(x, 0.0) - jnp.log1p(jnp.exp(-jnp.abs(x)))


def _mlstm_kernel(ua_ref, ub_ref, gt_ref, bias_ref, gm_ref, hm_ref, c_ref, n_ref, m_ref):
    L = M_CHUNK

    @pl.when(pl.program_id(1) == 0)
    def _():
        c_ref[...] = jnp.zeros_like(c_ref)
        n_ref[...] = jnp.zeros_like(n_ref)
        m_ref[...] = jnp.zeros_like(m_ref)

    lane = lax.broadcasted_iota(jnp.int32, (L, LANES), 1)
    row = lax.broadcasted_iota(jnp.int32, (L, L), 0)
    col = lax.broadcasted_iota(jnp.int32, (L, L), 1)
    causal = col <= row
    scale = HD ** -0.5

    g = gt_ref[...] + bias_ref[...]
    x = jnp.where(lane < N_HEADS, g, _log_sigmoid(g))
    cum = jnp.dot(causal.astype(F32), x, precision=lax.Precision.HIGHEST,
                  preferred_element_type=F32)
    x_t = x.T
    cum_t = cum.T
    m_all = m_ref[...]
    m_tile = jnp.zeros((1, LANES), F32)
    lane1 = lax.broadcasted_iota(jnp.int32, (1, LANES), 1)
    tn = (((0,), (0,)), ((), ()))
    for h in range(N_HEADS):
        ig_row = x_t[h:h + 1, :]
        ig_col = x[:, h:h + 1]
        b_row = cum_t[N_HEADS + h:N_HEADS + h + 1, :]
        b_col = cum[:, N_HEADS + h:N_HEADS + h + 1]
        m_st = m_all[:, h:h + 1]
        dlog = jnp.where(causal, b_col - b_row + ig_row, -jnp.inf)
        g_col = b_col + m_st
        m_row = jnp.maximum(g_col, jnp.max(dlog, axis=-1, keepdims=True))
        w_intra = jnp.exp(dlog - m_row)
        w_inter = jnp.exp(g_col - m_row)
        q = ua_ref[:, h * HD:(h + 1) * HD]
        k = ua_ref[:, M_QK + h * HD:M_QK + (h + 1) * HD]
        if h < 2:
            v = ua_ref[:, 2 * M_QK + h * DV:2 * M_QK + (h + 1) * DV]
        else:
            v = ub_ref[:, (h - 2) * DV:(h - 1) * DV]
        og = ub_ref[:, 2 * DV + h * DV:2 * DV + (h + 1) * DV]
        c_st = c_ref[h]
        n_st = n_ref[h:h + 1, :]
        s = lax.dot_general(q, k, NT_DIMS, preferred_element_type=F32) * (w_intra * scale)
        num = (jnp.dot(s.astype(BF16), v, preferred_element_type=F32)
               + w_inter * jnp.dot(q, c_st.astype(BF16), preferred_element_type=F32))
        den = (jnp.sum(s, axis=-1, keepdims=True)
               + w_inter * jnp.sum(q.astype(F32) * n_st, axis=-1, keepdims=True))
        hh = num / jnp.maximum(jnp.abs(den), jnp.exp(-m_row))
        hn = hh * lax.rsqrt(jnp.mean(hh * hh, axis=-1, keepdims=True) + EPS) * gm_ref[:, h * DV:(h + 1) * DV]
        hm_ref[:, h * DV:(h + 1) * DV] = (hn * jax.nn.sigmoid(og.astype(F32))).astype(hm_ref.dtype)
        b_last = b_col[L - 1:L, :]
        g_last = b_last + m_st
        wlog = b_last - b_col + ig_col
        m_new = jnp.maximum(g_last, jnp.max(wlog, axis=0, keepdims=True))
        w_s = jnp.exp(wlog - m_new) * scale
        decay = jnp.exp(g_last - m_new)
        kv = lax.dot_general(k, (w_s * v.astype(F32)).astype(BF16), tn, preferred_element_type=F32)
        c_ref[h] = decay * c_st + kv
        n_ref[h:h + 1, :] = decay * n_st + jnp.sum(w_s * k.astype(F32), axis=0, keepdims=True)
        m_tile = jnp.where(lane1 == h, m_new, m_tile)
    m_ref[...] = m_tile


def _mlstm_prompt(u_m, gates, gate_bias, g_mlstm, batch, seq):
    m = batch * seq
    L = M_CHUNK
    nc = seq // L
    half = U_MLSTM // 2
    return pl.pallas_call(
        _mlstm_kernel,
        grid=(batch, nc),
        in_specs=[
            pl.BlockSpec((L, half), lambda b, c: (b * nc + c, 0)),
            pl.BlockSpec((L, half), lambda b, c: (b * nc + c, 1)),
            pl.BlockSpec((L, LANES), lambda b, c: (b * nc + c, 0)),
            pl.BlockSpec((1, LANES), lambda b, c: (0, 0)),
            pl.BlockSpec((1, M_V), lambda b, c: (0, 0)),
        ],
        out_specs=[
            pl.BlockSpec((L, M_V), lambda b, c: (b * nc + c, 0)),
            pl.BlockSpec((None, N_HEADS, HD, DV), lambda b, c: (b, 0, 0, 0)),
            pl.BlockSpec((None, N_HEADS, HD), lambda b, c: (b, 0, 0)),
            pl.BlockSpec((None, 1, LANES), lambda b, c: (b, 0, 0)),
        ],
        out_shape=[
            jax.ShapeDtypeStruct((m, M_V), BF16),
            jax.ShapeDtypeStruct((batch, N_HEADS, HD, DV), F32),
            jax.ShapeDtypeStruct((batch, N_HEADS, HD), F32),
            jax.ShapeDtypeStruct((batch, 1, LANES), F32),
        ],
        compiler_params=_cparams(("parallel", "arbitrary")),
        name="mlstm_prompt",
    )(u_m, u_m, gates, gate_bias, g_mlstm.reshape(1, M_V))


def _mid_sample_kernel(qkv_ref, r_ref, k1_ref, k2_ref, k3_ref, st_ref, cw_ref, a_ref, co_ref, ns_ref):
    scale = HD ** -0.5
    caches = (k1_ref, k2_ref, k3_ref)
    outs, lses = [], []
    for g in range(N_GROUPS):
        base = 3 * g * N_HEADS
        q = qkv_ref[base:base + N_HEADS, :]
        k_new = qkv_ref[base + N_HEADS:base + 2 * N_HEADS, :]
        v_new = qkv_ref[base + 2 * N_HEADS:base + 3 * N_HEADS, :]
        k_c = caches[g][:, 0]
        v_c = caches[g][:, 1]
        s_c = jnp.sum(k_c * q[None], axis=-1, keepdims=True) * scale
        s_n = jnp.sum(k_new * q, axis=-1, keepdims=True) * scale
        mx = jnp.maximum(jnp.max(s_c, axis=0), s_n)
        p_c = jnp.exp(s_c - mx[None])
        p_n = jnp.exp(s_n - mx)
        den = jnp.sum(p_c, axis=0) + p_n
        outs.append((jnp.sum(p_c * v_c, axis=0) + p_n * v_new) / den)
        lses.append(mx + jnp.log(den))
    mx = jnp.maximum(jnp.maximum(lses[0], lses[1]), lses[2])
    e = [jnp.exp(l - mx) for l in lses]
    den = e[0] + e[1] + e[2]
    a = (e[0] / den) * outs[0] + (e[1] / den) * outs[1] + (e[2] / den) * outs[2]
    a_ref[...] = a.astype(a_ref.dtype)
    p = r_ref[:, 2 * W_C:3 * W_C].astype(F32) * r_ref[:, 0:W_C].astype(F32)
    z = cw_ref[0:1, :] * st_ref[0:1, :] + cw_ref[1:2, :] * st_ref[1:2, :] + cw_ref[2:3, :] * p
    co_ref[...] = (r_ref[:, W_C:2 * W_C].astype(F32) * z).astype(co_ref.dtype)
    ns_ref[0:1, :] = st_ref[1:2, :]
    ns_ref[1:2, :] = p


def _mid_sample(u_attn, u_rest, caches, state_conv, conv_w, layer):
    nb = u_attn.shape[0]
    views, cache_specs = [], []
    for g in range(N_GROUPS):
        dil = A_DILATIONS[g]
        c = caches[g]
        assert c.shape[2] == A_WINDOWS[g]
        views.append(c.reshape(c.shape[0], nb, A_BLK, dil, 2, N_HEADS, HD))
        cache_specs.append(pl.BlockSpec((None, None, A_BLK, None, 2, N_HEADS, HD),
                                        lambda b: (layer, b, 0, 0, 0, 0, 0)))
    return pl.pallas_call(
        _mid_sample_kernel,
        grid=(nb,),
        in_specs=[
            pl.BlockSpec((None, 9 * N_HEADS, HD), lambda b: (b, 0, 0)),
            pl.BlockSpec((None, 1, 3 * W_C), lambda b: (b, 0, 0)),
        ] + cache_specs + [
            pl.BlockSpec((None, None, CONV_W - 1, W_C), lambda b: (layer, b, 0, 0)),
            pl.BlockSpec((CONV_W, W_C), lambda b: (0, 0)),
        ],
        out_specs=[
            pl.BlockSpec((None, N_HEADS, HD), lambda b: (b, 0, 0)),
            pl.BlockSpec((None, 1, W_C), lambda b: (b, 0, 0)),
            pl.BlockSpec((None, CONV_W - 1, W_C), lambda b: (b, 0, 0)),
        ],
        out_shape=[
            jax.ShapeDtypeStruct((nb, N_HEADS, HD), BF16),
            jax.ShapeDtypeStruct((nb, 1, W_C), BF16),
            jax.ShapeDtypeStruct((nb, CONV_W - 1, W_C), F32),
        ],
        compiler_params=_cparams(("parallel",)),
        name="mid_sample",
    )(u_attn.reshape(nb, 9 * N_HEADS, HD), u_rest.reshape(nb, 1, -1), *views, state_conv, conv_w)


def _mlstm_step_kernel(u_ref, gt_ref, bias_ref, gm_ref, c_ref, n_ref, m_ref,
                       hm_ref, co_ref, no_ref, mo_ref):
    scale = HD ** -0.5
    g = gt_ref[...] + bias_ref[...]
    lf_all = _log_sigmoid(g)
    m_all = m_ref[...]
    lane1 = lax.broadcasted_iota(jnp.int32, (1, LANES), 1)
    eye = (lax.broadcasted_iota(jnp.int32, (HD, HD), 0)
           == lax.broadcasted_iota(jnp.int32, (HD, HD), 1))
    m_tile = jnp.zeros((1, LANES), F32)
    for h in range(N_HEADS):
        ig = g[:, h:h + 1]
        lf = lf_all[:, N_HEADS + h:N_HEADS + h + 1]
        m_st = m_all[:, h:h + 1]
        q = u_ref[:, h * HD:(h + 1) * HD]
        k = u_ref[:, M_QK + h * HD:M_QK + (h + 1) * HD]
        v = u_ref[:, 2 * M_QK + h * DV:2 * M_QK + (h + 1) * DV]
        og = u_ref[:, 2 * M_QK + M_V + h * DV:2 * M_QK + M_V + (h + 1) * DV]
        qf, kf, vf = q.astype(F32), k.astype(F32), v.astype(F32)
        c_st = c_ref[h]
        n_st = n_ref[h:h + 1, :]
        g_col = lf + m_st
        m_row = jnp.maximum(g_col, ig)
        w_intra = jnp.exp(ig - m_row)
        w_inter = jnp.exp(g_col - m_row)
        s = jnp.sum(qf * kf, axis=-1, keepdims=True) * (w_intra * scale)
        q_rows = jnp.broadcast_to(qf, (16, HD)).astype(BF16)
        qc = jnp.dot(q_rows, c_st.astype(BF16), preferred_element_type=F32)[0:1, :]
        num = s * vf + w_inter * qc
        den = s + w_inter * jnp.sum(qf * n_st, axis=-1, keepdims=True)
        hh = num / jnp.maximum(jnp.abs(den), jnp.exp(-m_row))
        hn = hh * lax.rsqrt(jnp.mean(hh * hh, axis=-1, keepdims=True) + EPS) * gm_ref[:, h * DV:(h + 1) * DV]
        hm_ref[:, h * DV:(h + 1) * DV] = (hn * jax.nn.sigmoid(og.astype(F32))).astype(hm_ref.dtype)
        w_s = w_intra * scale
        k_diag = jnp.where(eye, jnp.broadcast_to(kf, (HD, HD)), 0.0).astype(BF16)
        v_rows = jnp.broadcast_to(w_s * vf, (HD, DV)).astype(BF16)
        kv = jnp.dot(k_diag, v_rows, preferred_element_type=F32)
        co_ref[h] = w_inter * c_st + kv
        no_ref[h:h + 1, :] = w_inter * n_st + w_s * kf
        m_tile = jnp.where(lane1 == h, m_row, m_tile)
    mo_ref[...] = m_tile


def _mlstm_step(u_m, gates, gate_bias, g_mlstm, state_c, state_n, m_pad, layer):
    nb = u_m.shape[0]
    return pl.pallas_call(
        _mlstm_step_kernel,
        grid=(nb,),
        in_specs=[
            pl.BlockSpec((None, 1, U_MLSTM), lambda b: (b, 0, 0)),
            pl.BlockSpec((None, 1, LANES), lambda b: (b, 0, 0)),
            pl.BlockSpec((1, LANES), lambda b: (0, 0)),
            pl.BlockSpec((1, M_V), lambda b: (0, 0)),
            pl.BlockSpec((None, None, N_HEADS, HD, DV), lambda b: (layer, b, 0, 0, 0)),
            pl.BlockSpec((None, None, N_HEADS, HD), lambda b: (layer, b, 0, 0)),
            pl.BlockSpec((None, 1, LANES), lambda b: (b, 0, 0)),
        ],
        out_specs=[
            pl.BlockSpec((None, 1, M_V), lambda b: (b, 0, 0)),
            pl.BlockSpec((None, N_HEADS, HD, DV), lambda b: (b, 0, 0, 0)),
            pl.BlockSpec((None, N_HEADS, HD), lambda b: (b, 0, 0)),
            pl.BlockSpec((None, 1, LANES), lambda b: (b, 0, 0)),
        ],
        out_shape=[
            jax.ShapeDtypeStruct((nb, 1, M_V), BF16),
            jax.ShapeDtypeStruct((nb, N_HEADS, HD, DV), F32),
            jax.ShapeDtypeStruct((nb, N_HEADS, HD), F32),
            jax.ShapeDtypeStruct((nb, 1, LANES), F32),
        ],
        compiler_params=_cparams(("parallel",)),
        name="mlstm_step",
    )(u_m.reshape(nb, 1, U_MLSTM), gates.reshape(nb, 1, LANES), gate_bias,
      g_mlstm.reshape(1, M_V), state_c, state_n, m_pad)


def _pad_lanes(t):
    return jnp.pad(t, [(0, 0)] * (t.ndim - 1) + [(0, LANES - t.shape[-1])])


def _kv_rows(u_attn, group, lead_shape):
    k = u_attn[..., (3 * group + 1) * A_W:(3 * group + 2) * A_W]
    v = u_attn[..., (3 * group + 2) * A_W:(3 * group + 3) * A_W]
    return jnp.stack([k, v], axis=-2).reshape(lead_shape + (2, N_HEADS, HD))


def _run_group(x, mods, states, p, prompt):
    batch, seq, d = x.shape
    m = batch * seq
    depth = p["w_in_t"].shape[0]
    tm = 1024 if prompt else m
    xf = x.reshape(m, d)
    new_states = []
    for l in range(depth):
        sh1, sc1, gt1, sh2, sc2, gt2 = mods[l]
        u_attn, u_m, gates, u_rest = _in_proj(xf, p["g_norm1"][l], sh1, sc1, p["w_in_t"], l, tm)
        gate_bias = p["gate_bias"][l]
        if prompt:
            a_out = _attn_prompt(u_attn, batch, seq)
            c_out, new_conv = _conv_prompt(u_rest, p["conv_w"][l], batch, seq)
            hm, c_n, n_n, m_n = _mlstm_prompt(u_m, gates, gate_bias, p["g_mlstm"][l], batch, seq)
            u3 = u_attn.reshape(batch, seq, U_ATTN)
            bufs = []
            for g in range(N_GROUPS):
                keep = min(A_WINDOWS[g], seq)
                bufs.append(_kv_rows(u3[:, seq - keep:], g, (batch, keep)))
        else:
            caches, st_c, st_n, st_m, st_conv = states
            a_out, c_out, new_conv = _mid_sample(u_attn, u_rest[:, :3 * W_C], caches, st_conv,
                                                 p["conv_w"][l], l)
            a_out = a_out.reshape(m, A_W)
            c_out = c_out.reshape(m, W_C)
            hm, c_n, n_n, m_n = _mlstm_step(u_m, gates, gate_bias, p["g_mlstm"][l],
                                            st_c, st_n, _pad_lanes(st_m[l])[:, None, :], l)
            hm = hm.reshape(m, M_V)
            bufs = [_kv_rows(u_attn, g, (batch, seq)) for g in range(N_GROUPS)]
        merged = _branch_merge(a_out, hm, c_out, u_rest, p["w_br_a"], p["w_br_m"], p["w_br_c"], l, tm)
        xf = _mm_res(merged, p["w_mix_out"], l, xf, gt1, None, tm)
        act = _ffn_up(xf, p["g_norm2"][l], sh2, sc2, p["w_ffn_gate"], p["w_ffn_up"], l, tm)
        xf = _mm_res(act, p["w_ffn_down"], l, xf, gt2,
                     p["g_final"] if l == depth - 1 else None, tm)
        new_states.append(bufs + [c_n, n_n, m_n[:, 0, :N_HEADS], new_conv])
    stacked = [jnp.stack([n[i] for n in new_states], axis=0) for i in range(7)]
    return xf.reshape(batch, seq, d), stacked


def kernel(x_prompt, x_sample, cache_a1_kv, cache_a2_kv, cache_a3_kv, state_mlstm_c, state_mlstm_n,
           state_mlstm_m, state_conv, c_prompt, c_sample, w_ada, b_ada, g_norm1, g_norm2, w_in, b_igate,
           b_fgate, g_mlstm, conv_w, w_br_a, w_br_m, w_br_c, w_mix_out, w_ffn_gate, w_ffn_up, w_ffn_down,
           g_final):
    depth, d = g_norm1.shape
    nbp = c_prompt.shape[0]
    nbs = c_sample.shape[0]
    assert w_in.shape[-1] == U_REST_OFF + 3 * W_C + 3 * d

    c_all = jnp.concatenate([c_prompt, c_sample], axis=0)
    c_all = jnp.pad(c_all, ((0, (-c_all.shape[0]) % 8), (0, 0)))
    ada = _ada(c_all, w_ada, b_ada)
    mods_p, mods_s = [], []
    for l in range(depth):
        mods_p.append([ada[l, :nbp, i * d:(i + 1) * d].reshape(nbp, 1, d) for i in range(6)])
        mods_s.append([ada[l, nbp:nbp + nbs, i * d:(i + 1) * d].reshape(1, nbs, d) for i in range(6)])

    params = dict(
        g_norm1=g_norm1, g_norm2=g_norm2, g_mlstm=g_mlstm, conv_w=conv_w,
        w_br_a=w_br_a, w_br_m=w_br_m, w_br_c=w_br_c, w_mix_out=w_mix_out,
        w_ffn_gate=w_ffn_gate, w_ffn_up=w_ffn_up, w_ffn_down=w_ffn_down, g_final=g_final,
        w_in_t=jnp.swapaxes(w_in, 1, 2),
        gate_bias=_pad_lanes(jnp.concatenate([b_igate, b_fgate], axis=-1))[:, None, :],
    )

    y_p, ps = _run_group(x_prompt, mods_p, None, params, True)
    caches = (cache_a1_kv, cache_a2_kv, cache_a3_kv)
    states = (caches, state_mlstm_c, state_mlstm_n, state_mlstm_m, state_conv)
    y_s, ss = _run_group(x_sample, mods_s, states, params, False)
    seq_s = x_sample.shape[1]
    for g in range(N_GROUPS):
        ss[g] = jnp.concatenate([caches[g][:, :, seq_s:], ss[g]], axis=2)
    return (y_p, y_s, ps[0], ss[0], ps[1], ss[1], ps[2], ss[2], ps[3], ss[3], ps[4], ss[4],
            ps[5], ss[5], ps[6], ss[6])
```

```python
import functools

import jax
import jax.numpy as jnp
from jax import lax
from jax.experimental import pallas as pl
from jax.experimental.pallas import tpu as pltpu

F32 = jnp.float32
BF16 = jnp.bfloat16
EPS = 1e-6

A_WINDOWS = (128, 512, 2048)
A_DILATIONS = (1, 4, 16)
N_GROUPS = 3
HD = 128
A_BLK = 128
N_HEADS = 4
DV = 256
A_W = N_HEADS * HD
M_QK = N_HEADS * HD
M_V = N_HEADS * DV
W_C = 1024
CONV_W = 3
U_ATTN = 9 * A_W
U_MLSTM = 2 * M_QK + 2 * M_V
U_MAIN = U_ATTN + U_MLSTM
N_GATE_COLS = 2 * N_HEADS
U_REST_OFF = U_MAIN + N_GATE_COLS
LANES = 128
SUBLANES = 8
SEG = A_DILATIONS[-1] * A_BLK

VMEM_LIMIT = 52 * 1024 * 1024
NT_DIMS = (((1,), (1,)), ((), ()))


def _cparams(sem):
    return pltpu.CompilerParams(dimension_semantics=sem, vmem_limit_bytes=VMEM_LIMIT)


def _ada_kernel(c_ref, w_ref, b_ref, o_ref):
    c = c_ref[...]
    a = (c * jax.nn.sigmoid(c)).astype(BF16)
    o_ref[...] = jnp.dot(a, w_ref[...].astype(BF16), preferred_element_type=F32) + b_ref[...]


def _ada(c_all, w_ada, b_ada, tn=1024):
    depth, d, n = w_ada.shape
    r = c_all.shape[0]
    return pl.pallas_call(
        _ada_kernel,
        grid=(depth, n // tn),
        in_specs=[
            pl.BlockSpec((r, d), lambda l, j: (0, 0)),
            pl.BlockSpec((None, d, tn), lambda l, j: (l, 0, j)),
            pl.BlockSpec((None, 1, tn), lambda l, j: (l, 0, j)),
        ],
        out_specs=pl.BlockSpec((None, r, tn), lambda l, j: (l, 0, j)),
        out_shape=jax.ShapeDtypeStruct((depth, r, n), F32),
        compiler_params=_cparams(("parallel", "parallel")),
        name="ada",
    )(c_all, w_ada, b_ada.reshape(depth, 1, n))


def _modulated_norm(x_ref, g_ref, sh_ref, sc_ref, h_scr):
    x = x_ref[...]
    ms = jnp.mean(x * x, axis=-1, keepdims=True)
    y = x * lax.rsqrt(ms + EPS) * g_ref[...]
    h_scr[...] = (y * (1.0 + sc_ref[...]) + sh_ref[...]).astype(BF16)


def _row_specs(m, d, tm, shift):
    groups = shift.shape[0]
    blocks_per_group = (m // groups) // tm
    mod = pl.BlockSpec((None, shift.shape[1], d), lambda i, j: (i // blocks_per_group, 0, 0))
    return [
        pl.BlockSpec((tm, d), lambda i, j: (i, 0)),
        pl.BlockSpec((1, d), lambda i, j: (0, 0)),
        mod,
        mod,
    ]


IN_TN = 512
ATTN_TILES = U_ATTN // IN_TN
MAIN_TILES = U_MAIN // IN_TN


def _in_proj_kernel(x_ref, g_ref, sh_ref, sc_ref, w_ref, wg_ref, wn_ref,
                    oa_ref, om_ref, og_ref, or_ref, h_scr):
    j = pl.program_id(1)

    @pl.when(j == 0)
    def _():
        _modulated_norm(x_ref, g_ref, sh_ref, sc_ref, h_scr)
        wg = jnp.concatenate([wg_ref[...], jnp.zeros((LANES - N_GATE_COLS, wg_ref.shape[1]), F32)], axis=0)
        og_ref[...] = lax.dot_general(h_scr[...], wg.astype(BF16), NT_DIMS, preferred_element_type=F32)

    @pl.when(j < ATTN_TILES)
    def _():
        oa_ref[...] = lax.dot_general(h_scr[...], w_ref[...].astype(BF16), NT_DIMS,
                                      preferred_element_type=F32)

    @pl.when(jnp.logical_and(j >= ATTN_TILES, j < MAIN_TILES))
    def _():
        om_ref[...] = lax.dot_general(h_scr[...], w_ref[...].astype(BF16), NT_DIMS,
                                      preferred_element_type=F32).astype(om_ref.dtype)

    @pl.when(j >= MAIN_TILES)
    def _():
        w = jnp.concatenate([w_ref[N_GATE_COLS:, :], wn_ref[...]], axis=0).astype(BF16)
        or_ref[...] = lax.dot_general(h_scr[...], w, NT_DIMS,
                                      preferred_element_type=F32).astype(or_ref.dtype)


def _in_proj(x, g_norm, shift, scale, w_t, layer, tm):
    m, d = x.shape
    tn = IN_TN
    n_rest = w_t.shape[1] - U_REST_OFF
    n_tiles = MAIN_TILES + n_rest // tn
    assert N_GATE_COLS == SUBLANES and n_rest % tn == 0
    rows_per_tile = tn // SUBLANES
    return pl.pallas_call(
        _in_proj_kernel,
        grid=(m // tm, n_tiles),
        in_specs=_row_specs(m, d, tm, shift) + [
            pl.BlockSpec((None, tn, d), lambda i, j: (layer, j, 0)),
            pl.BlockSpec((None, SUBLANES, d), lambda i, j: (layer, U_MAIN // SUBLANES, 0)),
            pl.BlockSpec((None, SUBLANES, d),
                         lambda i, j: (layer, (jnp.maximum(j, MAIN_TILES) + 1) * rows_per_tile, 0)),
        ],
        out_specs=[
            pl.BlockSpec((tm, tn), lambda i, j: (i, jnp.minimum(j, ATTN_TILES - 1))),
            pl.BlockSpec((tm, tn), lambda i, j: (i, jnp.clip(j - ATTN_TILES, 0, MAIN_TILES - ATTN_TILES - 1))),
            pl.BlockSpec((tm, LANES), lambda i, j: (i, 0)),
            pl.BlockSpec((tm, tn), lambda i, j: (i, jnp.maximum(j - MAIN_TILES, 0))),
        ],
        out_shape=[
            jax.ShapeDtypeStruct((m, U_ATTN), F32),
            jax.ShapeDtypeStruct((m, U_MLSTM), BF16),
            jax.ShapeDtypeStruct((m, LANES), F32),
            jax.ShapeDtypeStruct((m, n_rest), BF16),
        ],
        scratch_shapes=[pltpu.VMEM((tm, d), BF16)],
        compiler_params=_cparams(("parallel", "arbitrary")),
        name="in_proj",
    )(x, g_norm.reshape(1, d), shift, scale, w_t, w_t, w_t)


def _ffn_up_kernel(x_ref, g_ref, sh_ref, sc_ref, wg_ref, wu_ref, o_ref, h_scr):
    @pl.when(pl.program_id(1) == 0)
    def _():
        _modulated_norm(x_ref, g_ref, sh_ref, sc_ref, h_scr)

    h = h_scr[...]
    a = jnp.dot(h, wg_ref[...].astype(BF16), preferred_element_type=F32)
    b = jnp.dot(h, wu_ref[...].astype(BF16), preferred_element_type=F32)
    o_ref[...] = (a * jax.nn.sigmoid(a) * b).astype(o_ref.dtype)


def _ffn_up(x, g_norm, shift, scale, w_gate, w_up, layer, tm, tn=512):
    m, d = x.shape
    d_ff = w_gate.shape[-1]
    wspec = pl.BlockSpec((None, d, tn), lambda i, j: (layer, 0, j))
    return pl.pallas_call(
        _ffn_up_kernel,
        grid=(m // tm, d_ff // tn),
        in_specs=_row_specs(m, d, tm, shift) + [wspec, wspec],
        out_specs=pl.BlockSpec((tm, tn), lambda i, j: (i, j)),
        out_shape=jax.ShapeDtypeStruct((m, d_ff), BF16),
        scratch_shapes=[pltpu.VMEM((tm, d), BF16)],
        compiler_params=_cparams(("parallel", "arbitrary")),
        name="ffn_up",
    )(x, g_norm.reshape(1, d), shift, scale, w_gate, w_up)


def _mm_res_kernel(a_ref, w_ref, res_ref, gt_ref, o_ref):
    y = jnp.dot(a_ref[...], w_ref[...].astype(BF16), preferred_element_type=F32)
    o_ref[...] = res_ref[...] + gt_ref[...] * y


def _mm_res(a, w, layer, res, gate, tm):
    m, kdim = a.shape
    d = res.shape[-1]
    tn = 512 if kdim <= 2048 else 256
    groups = gate.shape[0]
    blocks_per_group = (m // groups) // tm
    return pl.pallas_call(
        _mm_res_kernel,
        grid=(m // tm, d // tn),
        in_specs=[
            pl.BlockSpec((tm, kdim), lambda i, j: (i, 0)),
            pl.BlockSpec((None, kdim, tn), lambda i, j: (layer, 0, j)),
            pl.BlockSpec((tm, tn), lambda i, j: (i, j)),
            pl.BlockSpec((None, gate.shape[1], tn), lambda i, j: (i // blocks_per_group, 0, j)),
        ],
        out_specs=pl.BlockSpec((tm, tn), lambda i, j: (i, j)),
        out_shape=jax.ShapeDtypeStruct((m, d), F32),
        compiler_params=_cparams(("parallel", "parallel")),
        name="mm_res",
    )(a, w, res, gate)


def _final_norm_kernel(x_ref, g_ref, o_ref):
    x = x_ref[...]
    ms = jnp.mean(x * x, axis=-1, keepdims=True)
    o_ref[...] = x * lax.rsqrt(ms + EPS) * g_ref[...]


def _final_norm(x, g_final, tm):
    m, d = x.shape
    return pl.pallas_call(
        _final_norm_kernel,
        grid=(m // tm,),
        in_specs=[pl.BlockSpec((tm, d), lambda i: (i, 0)), pl.BlockSpec((1, d), lambda i: (0, 0))],
        out_specs=pl.BlockSpec((tm, d), lambda i: (i, 0)),
        out_shape=jax.ShapeDtypeStruct((m, d), F32),
        compiler_params=_cparams(("parallel",)),
        name="final_norm",
    )(x, g_final.reshape(1, d))


def _branch_kernel(a_ref, m_ref, c_ref, ga_ref, gm_ref, gc_ref, wa_ref, wm_ref, wc_ref, o_ref):
    def term(x_ref, w_ref, g_ref):
        y = jnp.dot(x_ref[...], w_ref[...].astype(BF16), preferred_element_type=F32)
        return jax.nn.sigmoid(g_ref[...].astype(F32)) * y

    o_ref[...] = (term(a_ref, wa_ref, ga_ref) + term(m_ref, wm_ref, gm_ref)
                  + term(c_ref, wc_ref, gc_ref)).astype(o_ref.dtype)


def _branch_merge(a_out, hm, c_out, u_rest, w_br_a, w_br_m, w_br_c, layer, tm, tn=512):
    m = a_out.shape[0]
    d = w_br_a.shape[-1]
    gate0 = (3 * W_C) // tn
    per_gate = d // tn

    def gate_spec(g):
        return pl.BlockSpec((tm, tn), lambda i, j: (i, gate0 + g * per_gate + j))

    def w_spec(k):
        return pl.BlockSpec((None, k, tn), lambda i, j: (layer, 0, j))

    return pl.pallas_call(
        _branch_kernel,
        grid=(m // tm, d // tn),
        in_specs=[
            pl.BlockSpec((tm, A_W), lambda i, j: (i, 0)),
            pl.BlockSpec((tm, M_V), lambda i, j: (i, 0)),
            pl.BlockSpec((tm, W_C), lambda i, j: (i, 0)),
            gate_spec(0), gate_spec(1), gate_spec(2),
            w_spec(A_W), w_spec(M_V), w_spec(W_C),
        ],
        out_specs=pl.BlockSpec((tm, tn), lambda i, j: (i, j)),
        out_shape=jax.ShapeDtypeStruct((m, d), BF16),
        compiler_params=_cparams(("parallel", "parallel")),
        name="branch_merge",
    )(a_out, hm, c_out, u_rest, u_rest, u_rest, w_br_a, w_br_m, w_br_c)


ATTN_NB = 4


def _attn_blocks(blocks):
    qi = lax.broadcasted_iota(jnp.int32, (A_BLK, A_BLK), 0)
    ki = lax.broadcasted_iota(jnp.int32, (A_BLK, A_BLK), 1)
    dist = ki - qi
    scale = HD ** -0.5
    n = len(blocks)
    qb = [b[0].astype(BF16) for b in blocks]
    s_cur = [lax.dot_general(qb[i], blocks[i][1].astype(BF16), NT_DIMS, preferred_element_type=F32)
             for i in range(n)]
    s_prev = [lax.dot_general(qb[i], blocks[i][2].astype(BF16), NT_DIMS, preferred_element_type=F32)
              for i in range(n)]
    s_cur = [jnp.where(dist <= 0, s * scale, -jnp.inf) for s in s_cur]
    s_prev = [jnp.where(dist >= blocks[i][5], s_prev[i] * scale, -jnp.inf) for i in range(n)]
    mx = [jnp.maximum(jnp.max(s_cur[i], axis=-1, keepdims=True), jnp.max(s_prev[i], axis=-1, keepdims=True))
          for i in range(n)]
    p_cur = [jnp.exp(s_cur[i] - mx[i]) for i in range(n)]
    p_prev = [jnp.exp(s_prev[i] - mx[i]) for i in range(n)]
    den = [jnp.sum(p_cur[i], axis=-1, keepdims=True) + jnp.sum(p_prev[i], axis=-1, keepdims=True)
           for i in range(n)]
    o = [jnp.dot(p_cur[i].astype(BF16), blocks[i][3].astype(BF16), preferred_element_type=F32)
         + jnp.dot(p_prev[i].astype(BF16), blocks[i][4].astype(BF16), preferred_element_type=F32)
         for i in range(n)]
    return [(o[i] / den[i], mx[i] + jnp.log(den[i])) for i in range(n)]


def _attn_kernel(*refs):
    ins = refs[:5 * N_GROUPS]
    a_ref = refs[5 * N_GROUPS]
    o_scr = refs[5 * N_GROUPS + 1:5 * N_GROUPS + 1 + N_GROUPS]
    l_scr = refs[5 * N_GROUPS + 1 + N_GROUPS:]
    nb = ATTN_NB
    first_lim = jnp.where(pl.program_id(1) > 0, 0, A_BLK)
    for g in range(N_GROUPS):
        d = A_DILATIONS[g]
        q_ref, kc_ref, kp_ref, vc_ref, vp_ref = ins[5 * g:5 * g + 5]
        span = d * A_BLK
        nblk = SEG // span

        def rows(start, d=d):
            return pl.ds(start, A_BLK, stride=d) if d > 1 else pl.ds(start, A_BLK)

        def head(r, rows=rows, q_ref=q_ref, kc_ref=kc_ref, kp_ref=kp_ref, vc_ref=vc_ref, vp_ref=vp_ref):
            return (q_ref[rows(r), :], kc_ref[rows(r), :], kp_ref[rows(r), :],
                    vc_ref[rows(r), :], vp_ref[rows(r), :], first_lim)

        def inner(start, rows=rows, span=span, q_ref=q_ref, kc_ref=kc_ref, vc_ref=vc_ref):
            return (q_ref[rows(start), :], kc_ref[rows(start), :], kc_ref[rows(start - span), :],
                    vc_ref[rows(start), :], vc_ref[rows(start - span), :], 0)

        def run(starts, blocks, g=g, rows=rows):
            for start, (o, lse) in zip(starts, _attn_blocks(blocks)):
                o_scr[g][rows(start), :] = o
                l_scr[g][rows(start), :] = jnp.broadcast_to(lse, (A_BLK, LANES))

        if nblk == 1:
            def body(t, c, run=run, head=head):
                rs = [t * nb + i for i in range(nb)]
                run(rs, [head(r) for r in rs])
                return c
            lax.fori_loop(0, d // nb, body, 0)
        else:
            assert nblk % nb == 0

            def body(r, c, run=run, head=head, inner=inner, span=span, nblk=nblk):
                starts = [r + i * span for i in range(nb)]
                run(starts, [head(r)] + [inner(s) for s in starts[1:]])

                def more(t, c2):
                    st = [r + (t * nb + i) * span for i in range(nb)]
                    run(st, [inner(s) for s in st])
                    return c2
                if nblk > nb:
                    lax.fori_loop(1, nblk // nb, more, 0)
                return c
            if d == 1:
                body(0, 0)
            else:
                lax.fori_loop(0, d, body, 0)

    chunk = 64

    def merge(c, carry):
        sl = pl.ds(pl.multiple_of(c * chunk, chunk), chunk)
        l0, l1, l2 = l_scr[0][sl, :], l_scr[1][sl, :], l_scr[2][sl, :]
        mx = jnp.maximum(jnp.maximum(l0, l1), l2)
        e0, e1, e2 = jnp.exp(l0 - mx), jnp.exp(l1 - mx), jnp.exp(l2 - mx)
        den = e0 + e1 + e2
        a = (e0 / den) * o_scr[0][sl, :] + (e1 / den) * o_scr[1][sl, :] + (e2 / den) * o_scr[2][sl, :]
        a_ref[sl, :] = a.astype(a_ref.dtype)
        return carry
    lax.fori_loop(0, SEG // chunk, merge, 0)


def _attn_prompt(u_attn, batch, seq):
    assert seq % SEG == 0
    nseg = seq // SEG
    specs = []
    for g in range(N_GROUPS):
        d = A_DILATIONS[g]
        assert A_WINDOWS[g] // d == A_BLK
        per_seg = SEG // (d * A_BLK)

        def cur(part, g=g):
            return pl.BlockSpec((SEG, HD), lambda b, s, h: (b * nseg + s, (3 * g + part) * N_HEADS + h))

        def prev(part, g=g, d=d, per_seg=per_seg):
            return pl.BlockSpec((d * A_BLK, HD),
                                lambda b, s, h: (jnp.maximum((b * nseg + s) * per_seg - 1, 0),
                                                 (3 * g + part) * N_HEADS + h))
        specs += [cur(0), cur(1), prev(1), cur(2), prev(2)]
    return pl.pallas_call(
        _attn_kernel,
        grid=(batch, nseg, N_HEADS),
        in_specs=specs,
        out_specs=pl.BlockSpec((SEG, HD), lambda b, s, h: (b * nseg + s, h)),
        out_shape=jax.ShapeDtypeStruct((batch * seq, A_W), BF16),
        scratch_shapes=[pltpu.VMEM((SEG, HD), F32)] * (2 * N_GROUPS),
        compiler_params=_cparams(("parallel", "parallel", "parallel")),
        name="attn_prompt",
    )(*([u_attn] * (5 * N_GROUPS)))


def _conv_prompt_kernel(blocks_per_seq, u_ref, b_ref, c_ref, cw_ref, co_ref, st_ref, p_scr):
    tm = u_ref.shape[0]
    i = pl.program_id(0)

    @pl.when(i % blocks_per_seq == 0)
    def _():
        p_scr[0:8, :] = jnp.zeros((8, p_scr.shape[1]), F32)

    @pl.when(i % blocks_per_seq != 0)
    def _():
        p_scr[0:8, :] = p_scr[tm:tm + 8, :]

    p = c_ref[...].astype(F32) * u_ref[...].astype(F32)
    p_scr[8:8 + tm, :] = p
    z = (cw_ref[0:1, :] * p_scr[6:6 + tm, :] + cw_ref[1:2, :] * p_scr[7:7 + tm, :]
         + cw_ref[2:3, :] * p)
    co_ref[...] = (b_ref[...].astype(F32) * z).astype(co_ref.dtype)
    st_ref[...] = p[tm - (CONV_W - 1):tm, :]


def _conv_prompt(u_rest, conv_w, batch, seq, tm=512):
    m = batch * seq
    blocks_per_seq = seq // tm
    return pl.pallas_call(
        functools.partial(_conv_prompt_kernel, blocks_per_seq),
        grid=(m // tm,),
        in_specs=[
            pl.BlockSpec((tm, W_C), lambda i: (i, 0)),
            pl.BlockSpec((tm, W_C), lambda i: (i, 1)),
            pl.BlockSpec((tm, W_C), lambda i: (i, 2)),
            pl.BlockSpec((CONV_W, W_C), lambda i: (0, 0)),
        ],
        out_specs=[
            pl.BlockSpec((tm, W_C), lambda i: (i, 0)),
            pl.BlockSpec((None, CONV_W - 1, W_C), lambda i: (i // blocks_per_seq, 0, 0)),
        ],
        out_shape=[
            jax.ShapeDtypeStruct((m, W_C), BF16),
            jax.ShapeDtypeStruct((batch, CONV_W - 1, W_C), F32),
        ],
        scratch_shapes=[pltpu.VMEM((tm + 8, W_C), F32)],
        compiler_params=_cparams(("arbitrary",)),
        name="conv_prompt",
    )(u_rest, u_rest, u_rest, conv_w)


M_CHUNK = 128


def _log_sigmoid(x):
    return jnp.minimum(x, 0.0) - jnp.log1p(jnp.exp(-jnp.abs(x)))


def _mlstm_kernel(ua_ref, ub_ref, gt_ref, bias_ref, gm_ref, hm_ref, c_ref, n_ref, m_ref):
    L = M_CHUNK

    @pl.when(pl.program_id(1) == 0)
    def _():
        c_ref[...] = jnp.zeros_like(c_ref)
        n_ref[...] = jnp.zeros_like(n_ref)
        m_ref[...] = jnp.zeros_like(m_ref)

    lane = lax.broadcasted_iota(jnp.int32, (L, LANES), 1)
    row = lax.broadcasted_iota(jnp.int32, (L, L), 0)
    col = lax.broadcasted_iota(jnp.int32, (L, L), 1)
    causal = col <= row
    scale = HD ** -0.5

    g = gt_ref[...] + bias_ref[...]
    x = jnp.where(lane < N_HEADS, g, _log_sigmoid(g))
    cum = jnp.dot(causal.astype(F32), x, precision=lax.Precision.HIGHEST,
                  preferred_element_type=F32)
    x_t = x.T
    cum_t = cum.T
    m_all = m_ref[...]
    m_tile = jnp.zeros((1, LANES), F32)
    lane1 = lax.broadcasted_iota(jnp.int32, (1, LANES), 1)
    tn = (((0,), (0,)), ((), ()))
    for h in range(N_HEADS):
        ig_row = x_t[h:h + 1, :]
        ig_col = x[:, h:h + 1]
        b_row = cum_t[N_HEADS + h:N_HEADS + h + 1, :]
        b_col = cum[:, N_HEADS + h:N_HEADS + h + 1]
        m_st = m_all[:, h:h + 1]
        dlog = jnp.where(causal, b_col - b_row + ig_row, -jnp.inf)
        g_col = b_col + m_st
        m_row = jnp.maximum(g_col, jnp.max(dlog, axis=-1, keepdims=True))
        w_intra = jnp.exp(dlog - m_row)
        w_inter = jnp.exp(g_col - m_row)
        q = ua_ref[:, h * HD:(h + 1) * HD]
        k = ua_ref[:, M_QK + h * HD:M_QK + (h + 1) * HD]
        if h < 2:
            v = ua_ref[:, 2 * M_QK + h * DV:2 * M_QK + (h + 1) * DV]
        else:
            v = ub_ref[:, (h - 2) * DV:(h - 1) * DV]
        og = ub_ref[:, 2 * DV + h * DV:2 * DV + (h + 1) * DV]
        c_st = c_ref[h]
        n_st = n_ref[h:h + 1, :]
        s = lax.dot_general(q, k, NT_DIMS, preferred_element_type=F32) * (w_intra * scale)
        num = (jnp.dot(s.astype(BF16), v, preferred_element_type=F32)
               + w_inter * jnp.dot(q, c_st.astype(BF16), preferred_element_type=F32))
        den = (jnp.sum(s, axis=-1, keepdims=True)
               + w_inter * jnp.sum(q.astype(F32) * n_st, axis=-1, keepdims=True))
        hh = num / jnp.maximum(jnp.abs(den), jnp.exp(-m_row))
        hn = hh * lax.rsqrt(jnp.mean(hh * hh, axis=-1, keepdims=True) + EPS) * gm_ref[:, h * DV:(h + 1) * DV]
        hm_ref[:, h * DV:(h + 1) * DV] = (hn * jax.nn.sigmoid(og.astype(F32))).astype(hm_ref.dtype)
        b_last = b_col[L - 1:L, :]
        g_last = b_last + m_st
        wlog = b_last - b_col + ig_col
        m_new = jnp.maximum(g_last, jnp.max(wlog, axis=0, keepdims=True))
        w_s = jnp.exp(wlog - m_new) * scale
        decay = jnp.exp(g_last - m_new)
        kv = lax.dot_general(k, (w_s * v.astype(F32)).astype(BF16), tn, preferred_element_type=F32)
        c_ref[h] = decay * c_st + kv
        n_ref[h:h + 1, :] = decay * n_st + jnp.sum(w_s * k.astype(F32), axis=0, keepdims=True)
        m_tile = jnp.where(lane1 == h, m_new, m_tile)
    m_ref[...] = m_tile


def _mlstm_prompt(u_m, gates, gate_bias, g_mlstm, batch, seq):
    m = batch * seq
    L = M_CHUNK
    nc = seq // L
    half = U_MLSTM // 2
    return pl.pallas_call(
        _mlstm_kernel,
        grid=(batch, nc),
        in_specs=[
            pl.BlockSpec((L, half), lambda b, c: (b * nc + c, 0)),
            pl.BlockSpec((L, half), lambda b, c: (b * nc + c, 1)),
            pl.BlockSpec((L, LANES), lambda b, c: (b * nc + c, 0)),
            pl.BlockSpec((1, LANES), lambda b, c: (0, 0)),
            pl.BlockSpec((1, M_V), lambda b, c: (0, 0)),
        ],
        out_specs=[
            pl.BlockSpec((L, M_V), lambda b, c: (b * nc + c, 0)),
            pl.BlockSpec((None, N_HEADS, HD, DV), lambda b, c: (b, 0, 0, 0)),
            pl.BlockSpec((None, N_HEADS, HD), lambda b, c: (b, 0, 0)),
            pl.BlockSpec((None, 1, LANES), lambda b, c: (b, 0, 0)),
        ],
        out_shape=[
            jax.ShapeDtypeStruct((m, M_V), BF16),
            jax.ShapeDtypeStruct((batch, N_HEADS, HD, DV), F32),
            jax.ShapeDtypeStruct((batch, N_HEADS, HD), F32),
            jax.ShapeDtypeStruct((batch, 1, LANES), F32),
        ],
        compiler_params=_cparams(("parallel", "arbitrary")),
        name="mlstm_prompt",
    )(u_m, u_m, gates, gate_bias, g_mlstm.reshape(1, M_V))


def _mid_sample_kernel(qkv_ref, r_ref, k1_ref, k2_ref, k3_ref, st_ref, cw_ref, a_ref, co_ref, ns_ref):
    scale = HD ** -0.5
    caches = (k1_ref, k2_ref, k3_ref)
    outs, lses = [], []
    for g in range(N_GROUPS):
        base = 3 * g * N_HEADS
        q = qkv_ref[base:base + N_HEADS, :]
        k_new = qkv_ref[base + N_HEADS:base + 2 * N_HEADS, :]
        v_new = qkv_ref[base + 2 * N_HEADS:base + 3 * N_HEADS, :]
        k_c = caches[g][:, 0]
        v_c = caches[g][:, 1]
        s_c = jnp.sum(k_c * q[None], axis=-1, keepdims=True) * scale
        s_n = jnp.sum(k_new * q, axis=-1, keepdims=True) * scale
        mx = jnp.maximum(jnp.max(s_c, axis=0), s_n)
        p_c = jnp.exp(s_c - mx[None])
        p_n = jnp.exp(s_n - mx)
        den = jnp.sum(p_c, axis=0) + p_n
        outs.append((jnp.sum(p_c * v_c, axis=0) + p_n * v_new) / den)
        lses.append(mx + jnp.log(den))
    mx = jnp.maximum(jnp.maximum(lses[0], lses[1]), lses[2])
    e = [jnp.exp(l - mx) for l in lses]
    den = e[0] + e[1] + e[2]
    a = (e[0] / den) * outs[0] + (e[1] / den) * outs[1] + (e[2] / den) * outs[2]
    a_ref[...] = a.astype(a_ref.dtype)
    p = r_ref[:, 2 * W_C:3 * W_C].astype(F32) * r_ref[:, 0:W_C].astype(F32)
    z = cw_ref[0:1, :] * st_ref[0:1, :] + cw_ref[1:2, :] * st_ref[1:2, :] + cw_ref[2:3, :] * p
    co_ref[...] = (r_ref[:, W_C:2 * W_C].astype(F32) * z).astype(co_ref.dtype)
    ns_ref[0:1, :] = st_ref[1:2, :]
    ns_ref[1:2, :] = p


def _mid_sample(u_attn, u_rest, caches, state_conv, conv_w, layer):
    nb = u_attn.shape[0]
    views, cache_specs = [], []
    for g in range(N_GROUPS):
        dil = A_DILATIONS[g]
        c = caches[g]
        assert c.shape[2] == A_WINDOWS[g]
        views.append(c.reshape(c.shape[0], nb, A_BLK, dil, 2, N_HEADS, HD))
        cache_specs.append(pl.BlockSpec((None, None, A_BLK, None, 2, N_HEADS, HD),
                                        lambda b: (layer, b, 0, 0, 0, 0, 0)))
    return pl.pallas_call(
        _mid_sample_kernel,
        grid=(nb,),
        in_specs=[
            pl.BlockSpec((None, 9 * N_HEADS, HD), lambda b: (b, 0, 0)),
            pl.BlockSpec((None, 1, 3 * W_C), lambda b: (b, 0, 0)),
        ] + cache_specs + [
            pl.BlockSpec((None, None, CONV_W - 1, W_C), lambda b: (layer, b, 0, 0)),
            pl.BlockSpec((CONV_W, W_C), lambda b: (0, 0)),
        ],
        out_specs=[
            pl.BlockSpec((None, N_HEADS, HD), lambda b: (b, 0, 0)),
            pl.BlockSpec((None, 1, W_C), lambda b: (b, 0, 0)),
            pl.BlockSpec((None, CONV_W - 1, W_C), lambda b: (b, 0, 0)),
        ],
        out_shape=[
            jax.ShapeDtypeStruct((nb, N_HEADS, HD), BF16),
            jax.ShapeDtypeStruct((nb, 1, W_C), BF16),
            jax.ShapeDtypeStruct((nb, CONV_W - 1, W_C), F32),
        ],
        compiler_params=_cparams(("parallel",)),
        name="mid_sample",
    )(u_attn.reshape(nb, 9 * N_HEADS, HD), u_rest.reshape(nb, 1, -1), *views, state_conv, conv_w)


def _mlstm_step_kernel(u_ref, gt_ref, bias_ref, gm_ref, c_ref, n_ref, m_ref,
                       hm_ref, co_ref, no_ref, mo_ref):
    scale = HD ** -0.5
    g = gt_ref[...] + bias_ref[...]
    lf_all = _log_sigmoid(g)
    m_all = m_ref[...]
    lane1 = lax.broadcasted_iota(jnp.int32, (1, LANES), 1)
    eye = (lax.broadcasted_iota(jnp.int32, (HD, HD), 0)
           == lax.broadcasted_iota(jnp.int32, (HD, HD), 1))
    m_tile = jnp.zeros((1, LANES), F32)
    for h in range(N_HEADS):
        ig = g[:, h:h + 1]
        lf = lf_all[:, N_HEADS + h:N_HEADS + h + 1]
        m_st = m_all[:, h:h + 1]
        q = u_ref[:, h * HD:(h + 1) * HD]
        k = u_ref[:, M_QK + h * HD:M_QK + (h + 1) * HD]
        v = u_ref[:, 2 * M_QK + h * DV:2 * M_QK + (h + 1) * DV]
        og = u_ref[:, 2 * M_QK + M_V + h * DV:2 * M_QK + M_V + (h + 1) * DV]
        qf, kf, vf = q.astype(F32), k.astype(F32), v.astype(F32)
        c_st = c_ref[h]
        n_st = n_ref[h:h + 1, :]
        g_col = lf + m_st
        m_row = jnp.maximum(g_col, ig)
        w_intra = jnp.exp(ig - m_row)
        w_inter = jnp.exp(g_col - m_row)
        s = jnp.sum(qf * kf, axis=-1, keepdims=True) * (w_intra * scale)
        q_rows = jnp.broadcast_to(qf, (16, HD)).astype(BF16)
        qc = jnp.dot(q_rows, c_st.astype(BF16), preferred_element_type=F32)[0:1, :]
        num = s * vf + w_inter * qc
        den = s + w_inter * jnp.sum(qf * n_st, axis=-1, keepdims=True)
        hh = num / jnp.maximum(jnp.abs(den), jnp.exp(-m_row))
        hn = hh * lax.rsqrt(jnp.mean(hh * hh, axis=-1, keepdims=True) + EPS) * gm_ref[:, h * DV:(h + 1) * DV]
        hm_ref[:, h * DV:(h + 1) * DV] = (hn * jax.nn.sigmoid(og.astype(F32))).astype(hm_ref.dtype)
        w_s = w_intra * scale
        k_diag = jnp.where(eye, jnp.broadcast_to(kf, (HD, HD)), 0.0).astype(BF16)
        v_rows = jnp.broadcast_to(w_s * vf, (HD, DV)).astype(BF16)
        kv = jnp.dot(k_diag, v_rows, preferred_element_type=F32)
        co_ref[h] = w_inter * c_st + kv
        no_ref[h:h + 1, :] = w_inter * n_st + w_s * kf
        m_tile = jnp.where(lane1 == h, m_row, m_tile)
    mo_ref[...] = m_tile


def _mlstm_step(u_m, gates, gate_bias, g_mlstm, state_c, state_n, m_pad, layer):
    nb = u_m.shape[0]
    return pl.pallas_call(
        _mlstm_step_kernel,
        grid=(nb,),
        in_specs=[
            pl.BlockSpec((None, 1, U_MLSTM), lambda b: (b, 0, 0)),
            pl.BlockSpec((None, 1, LANES), lambda b: (b, 0, 0)),
            pl.BlockSpec((1, LANES), lambda b: (0, 0)),
            pl.BlockSpec((1, M_V), lambda b: (0, 0)),
            pl.BlockSpec((None, None, N_HEADS, HD, DV), lambda b: (layer, b, 0, 0, 0)),
            pl.BlockSpec((None, None, N_HEADS, HD), lambda b: (layer, b, 0, 0)),
            pl.BlockSpec((None, 1, LANES), lambda b: (b, 0, 0)),
        ],
        out_specs=[
            pl.BlockSpec((None, 1, M_V), lambda b: (b, 0, 0)),
            pl.BlockSpec((None, N_HEADS, HD, DV), lambda b: (b, 0, 0, 0)),
            pl.BlockSpec((None, N_HEADS, HD), lambda b: (b, 0, 0)),
            pl.BlockSpec((None, 1, LANES), lambda b: (b, 0, 0)),
        ],
        out_shape=[
            jax.ShapeDtypeStruct((nb, 1, M_V), BF16),
            jax.ShapeDtypeStruct((nb, N_HEADS, HD, DV), F32),
            jax.ShapeDtypeStruct((nb, N_HEADS, HD), F32),
            jax.ShapeDtypeStruct((nb, 1, LANES), F32),
        ],
        compiler_params=_cparams(("parallel",)),
        name="mlstm_step",
    )(u_m.reshape(nb, 1, U_MLSTM), gates.reshape(nb, 1, LANES), gate_bias,
      g_mlstm.reshape(1, M_V), state_c, state_n, m_pad)


def _pad_lanes(t):
    return jnp.pad(t, [(0, 0)] * (t.ndim - 1) + [(0, LANES - t.shape[-1])])


def _kv_rows(u_attn, group, lead_shape):
    k = u_attn[..., (3 * group + 1) * A_W:(3 * group + 2) * A_W]
    v = u_attn[..., (3 * group + 2) * A_W:(3 * group + 3) * A_W]
    return jnp.stack([k, v], axis=-2).reshape(lead_shape + (2, N_HEADS, HD))


def _run_group(x, mods, states, p, prompt):
    batch, seq, d = x.shape
    m = batch * seq
    depth = p["w_in_t"].shape[0]
    tm = 1024 if prompt else m
    xf = x.reshape(m, d)
    new_states = []
    for l in range(depth):
        sh1, sc1, gt1, sh2, sc2, gt2 = mods[l]
        u_attn, u_m, gates, u_rest = _in_proj(xf, p["g_norm1"][l], sh1, sc1, p["w_in_t"], l, tm)
        gate_bias = p["gate_bias"][l]
        if prompt:
            a_out = _attn_prompt(u_attn, batch, seq)
            c_out, new_conv = _conv_prompt(u_rest, p["conv_w"][l], batch, seq)
            hm, c_n, n_n, m_n = _mlstm_prompt(u_m, gates, gate_bias, p["g_mlstm"][l], batch, seq)
            u3 = u_attn.reshape(batch, seq, U_ATTN)
            bufs = []
            for g in range(N_GROUPS):
                keep = min(A_WINDOWS[g], seq)
                bufs.append(_kv_rows(u3[:, seq - keep:], g, (batch, keep)))
        else:
            caches, st_c, st_n, st_m, st_conv = states
            a_out, c_out, new_conv = _mid_sample(u_attn, u_rest[:, :3 * W_C], caches, st_conv,
                                                 p["conv_w"][l], l)
            a_out = a_out.reshape(m, A_W)
            c_out = c_out.reshape(m, W_C)
            hm, c_n, n_n, m_n = _mlstm_step(u_m, gates, gate_bias, p["g_mlstm"][l],
                                            st_c, st_n, _pad_lanes(st_m[l])[:, None, :], l)
            hm = hm.reshape(m, M_V)
            bufs = [_kv_rows(u_attn, g, (batch, seq)) for g in range(N_GROUPS)]
        merged = _branch_merge(a_out, hm, c_out, u_rest, p["w_br_a"], p["w_br_m"], p["w_br_c"], l, tm)
        xf = _mm_res(merged, p["w_mix_out"], l, xf, gt1, tm)
        act = _ffn_up(xf, p["g_norm2"][l], sh2, sc2, p["w_ffn_gate"], p["w_ffn_up"], l, tm)
        xf = _mm_res(act, p["w_ffn_down"], l, xf, gt2, tm)
        new_states.append(bufs + [c_n, n_n, m_n[:, 0, :N_HEADS], new_conv])
    y = _final_norm(xf, p["g_final"], min(m, 512))
    stacked = [jnp.stack([n[i] for n in new_states], axis=0) for i in range(7)]
    return y.reshape(batch, seq, d), stacked


def kernel(x_prompt, x_sample, cache_a1_kv, cache_a2_kv, cache_a3_kv, state_mlstm_c, state_mlstm_n,
           state_mlstm_m, state_conv, c_prompt, c_sample, w_ada, b_ada, g_norm1, g_norm2, w_in, b_igate,
           b_fgate, g_mlstm, conv_w, w_br_a, w_br_m, w_br_c, w_mix_out, w_ffn_gate, w_ffn_up, w_ffn_down,
           g_final):
    depth, d = g_norm1.shape
    nbp = c_prompt.shape[0]
    nbs = c_sample.shape[0]
    assert w_in.shape[-1] == U_REST_OFF + 3 * W_C + 3 * d

    c_all = jnp.concatenate([c_prompt, c_sample], axis=0)
    c_all = jnp.pad(c_all, ((0, (-c_all.shape[0]) % 8), (0, 0)))
    ada = _ada(c_all, w_ada, b_ada)
    mods_p, mods_s = [], []
    for l in range(depth):
        mods_p.append([ada[l, :nbp, i * d:(i + 1) * d].reshape(nbp, 1, d) for i in range(6)])
        mods_s.append([ada[l, nbp:nbp + nbs, i * d:(i + 1) * d].reshape(1, nbs, d) for i in range(6)])

    params = dict(
        g_norm1=g_norm1, g_norm2=g_norm2, g_mlstm=g_mlstm, conv_w=conv_w,
        w_br_a=w_br_a, w_br_m=w_br_m, w_br_c=w_br_c, w_mix_out=w_mix_out,
        w_ffn_gate=w_ffn_gate, w_ffn_up=w_ffn_up, w_ffn_down=w_ffn_down, g_final=g_final,
        w_in_t=jnp.swapaxes(w_in, 1, 2),
        gate_bias=_pad_lanes(jnp.concatenate([b_igate, b_fgate], axis=-1))[:, None, :],
    )

    y_p, ps = _run_group(x_prompt, mods_p, None, params, True)
    caches = (cache_a1_kv, cache_a2_kv, cache_a3_kv)
    states = (caches, state_mlstm_c, state_mlstm_n, state_mlstm_m, state_conv)
    y_s, ss = _run_group(x_sample, mods_s, states, params, False)
    seq_s = x_sample.shape[1]
    for g in range(N_GROUPS):
        win = caches[g].shape[2]
        keep = [(0, 0, 0)] * 2
        tail = [(0, 0, 0)] * 3
        zero = jnp.zeros((), F32)
        shifted = lax.pad(caches[g], zero, keep + [(-seq_s, seq_s, 0)] + tail)
        ss[g] = shifted + lax.pad(ss[g], zero, keep + [(win - seq_s, 0, 0)] + tail)
    return (y_p, y_s, ps[0], ss[0], ps[1], ss[1], ps[2], ss[2], ps[3], ss[3], ps[4], ss[4],
            ps[5], ss[5], ps[6], ss[6])
```

```python
import functools

import jax
import jax.numpy as jnp
from jax import lax
from jax.experimental import pallas as pl
from jax.experimental.pallas import tpu as pltpu

F32 = jnp.float32
BF16 = jnp.bfloat16
EPS = 1e-6

A_WINDOWS = (128, 512, 2048)
A_DILATIONS = (1, 4, 16)
N_GROUPS = 3
HD = 128
A_BLK = 128
N_HEADS = 4
DV = 256
A_W = N_HEADS * HD
M_QK = N_HEADS * HD
M_V = N_HEADS * DV
W_C = 1024
CONV_W = 3
U_ATTN = 9 * A_W
U_MLSTM = 2 * M_QK + 2 * M_V
U_MAIN = U_ATTN + U_MLSTM
N_GATE_COLS = 2 * N_HEADS
U_REST_OFF = U_MAIN + N_GATE_COLS
LANES = 128
SUBLANES = 8
SEG = A_DILATIONS[-1] * A_BLK

VMEM_LIMIT = 52 * 1024 * 1024
NT_DIMS = (((1,), (1,)), ((), ()))


def _cparams(sem):
    return pltpu.CompilerParams(dimension_semantics=sem, vmem_limit_bytes=VMEM_LIMIT)


def _ada_kernel(c_ref, w_ref, b_ref, o_ref):
    c = c_ref[...]
    a = (c * jax.nn.sigmoid(c)).astype(BF16)
    o_ref[...] = jnp.dot(a, w_ref[...].astype(BF16), preferred_element_type=F32) + b_ref[...]


def _ada(c_all, w_ada, b_ada, tn=1024):
    depth, d, n = w_ada.shape
    r = c_all.shape[0]
    return pl.pallas_call(
        _ada_kernel,
        grid=(depth, n // tn),
        in_specs=[
            pl.BlockSpec((r, d), lambda l, j: (0, 0)),
            pl.BlockSpec((None, d, tn), lambda l, j: (l, 0, j)),
            pl.BlockSpec((None, 1, tn), lambda l, j: (l, 0, j)),
        ],
        out_specs=pl.BlockSpec((None, r, tn), lambda l, j: (l, 0, j)),
        out_shape=jax.ShapeDtypeStruct((depth, r, n), F32),
        compiler_params=_cparams(("parallel", "parallel")),
        name="ada",
    )(c_all, w_ada, b_ada.reshape(depth, 1, n))


def _modulated_norm(x_ref, g_ref, sh_ref, sc_ref, h_scr):
    x = x_ref[...]
    ms = jnp.mean(x * x, axis=-1, keepdims=True)
    y = x * lax.rsqrt(ms + EPS) * g_ref[...]
    h_scr[...] = (y * (1.0 + sc_ref[...]) + sh_ref[...]).astype(BF16)


def _row_specs(m, d, tm, shift):
    groups = shift.shape[0]
    blocks_per_group = (m // groups) // tm
    mod = pl.BlockSpec((None, shift.shape[1], d), lambda i, j: (i // blocks_per_group, 0, 0))
    return [
        pl.BlockSpec((tm, d), lambda i, j: (i, 0)),
        pl.BlockSpec((1, d), lambda i, j: (0, 0)),
        mod,
        mod,
    ]


IN_TN = 512
ATTN_TILES = U_ATTN // IN_TN
MAIN_TILES = U_MAIN // IN_TN


def _w_in_cast_kernel(w_ref, wn_ref, o_ref):
    j = pl.program_id(1)

    @pl.when(j < MAIN_TILES)
    def _():
        o_ref[...] = w_ref[...].astype(o_ref.dtype)

    @pl.when(j >= MAIN_TILES)
    def _():
        o_ref[...] = jnp.concatenate([w_ref[N_GATE_COLS:, :], wn_ref[...]], axis=0).astype(o_ref.dtype)


def _w_in_cast(w_t):
    depth, n, d = w_t.shape
    tn = IN_TN
    n_rest = n - U_REST_OFF
    n_tiles = MAIN_TILES + n_rest // tn
    assert N_GATE_COLS == SUBLANES and n_rest % tn == 0
    rows_per_tile = tn // SUBLANES
    return pl.pallas_call(
        _w_in_cast_kernel,
        grid=(depth, n_tiles),
        in_specs=[
            pl.BlockSpec((None, tn, d), lambda l, j: (l, j, 0)),
            pl.BlockSpec((None, SUBLANES, d),
                         lambda l, j: (l, (jnp.maximum(j, MAIN_TILES) + 1) * rows_per_tile, 0)),
        ],
        out_specs=pl.BlockSpec((None, tn, d), lambda l, j: (l, j, 0)),
        out_shape=jax.ShapeDtypeStruct((depth, n_tiles * tn, d), BF16),
        compiler_params=_cparams(("parallel", "parallel")),
        name="w_in_cast",
    )(w_t, w_t)


def _in_proj_kernel(x_ref, g_ref, sh_ref, sc_ref, w_ref, wg_ref,
                    oa_ref, om_ref, og_ref, or_ref, h_scr):
    j = pl.program_id(1)

    @pl.when(j == 0)
    def _():
        _modulated_norm(x_ref, g_ref, sh_ref, sc_ref, h_scr)
        wg = jnp.concatenate([wg_ref[...], jnp.zeros((LANES - N_GATE_COLS, wg_ref.shape[1]), F32)], axis=0)
        og_ref[...] = lax.dot_general(h_scr[...], wg.astype(BF16), NT_DIMS, preferred_element_type=F32)

    @pl.when(j < ATTN_TILES)
    def _():
        oa_ref[...] = lax.dot_general(h_scr[...], w_ref[...], NT_DIMS, preferred_element_type=F32)

    @pl.when(jnp.logical_and(j >= ATTN_TILES, j < MAIN_TILES))
    def _():
        om_ref[...] = lax.dot_general(h_scr[...], w_ref[...], NT_DIMS,
                                      preferred_element_type=F32).astype(om_ref.dtype)

    @pl.when(j >= MAIN_TILES)
    def _():
        or_ref[...] = lax.dot_general(h_scr[...], w_ref[...], NT_DIMS,
                                      preferred_element_type=F32).astype(or_ref.dtype)


def _in_proj(x, g_norm, shift, scale, w_t, w_bf, layer, tm):
    m, d = x.shape
    tn = IN_TN
    n_tiles = w_bf.shape[1] // tn
    n_rest = (n_tiles - MAIN_TILES) * tn
    return pl.pallas_call(
        _in_proj_kernel,
        grid=(m // tm, n_tiles),
        in_specs=_row_specs(m, d, tm, shift) + [
            pl.BlockSpec((None, tn, d), lambda i, j: (layer, j, 0)),
            pl.BlockSpec((None, SUBLANES, d), lambda i, j: (layer, U_MAIN // SUBLANES, 0)),
        ],
        out_specs=[
            pl.BlockSpec((tm, tn), lambda i, j: (i, jnp.minimum(j, ATTN_TILES - 1))),
            pl.BlockSpec((tm, tn), lambda i, j: (i, jnp.clip(j - ATTN_TILES, 0, MAIN_TILES - ATTN_TILES - 1))),
            pl.BlockSpec((tm, LANES), lambda i, j: (i, 0)),
            pl.BlockSpec((tm, tn), lambda i, j: (i, jnp.maximum(j - MAIN_TILES, 0))),
        ],
        out_shape=[
            jax.ShapeDtypeStruct((m, U_ATTN), F32),
            jax.ShapeDtypeStruct((m, U_MLSTM), BF16),
            jax.ShapeDtypeStruct((m, LANES), F32),
            jax.ShapeDtypeStruct((m, n_rest), BF16),
        ],
        scratch_shapes=[pltpu.VMEM((tm, d), BF16)],
        compiler_params=_cparams(("parallel", "arbitrary")),
        name="in_proj",
    )(x, g_norm.reshape(1, d), shift, scale, w_bf, w_t)


def _ffn_up_kernel(x_ref, g_ref, sh_ref, sc_ref, wg_ref, wu_ref, o_ref, h_scr):
    @pl.when(pl.program_id(1) == 0)
    def _():
        _modulated_norm(x_ref, g_ref, sh_ref, sc_ref, h_scr)

    h = h_scr[...]
    a = jnp.dot(h, wg_ref[...].astype(BF16), preferred_element_type=F32)
    b = jnp.dot(h, wu_ref[...].astype(BF16), preferred_element_type=F32)
    o_ref[...] = (a * jax.nn.sigmoid(a) * b).astype(o_ref.dtype)


def _ffn_up(x, g_norm, shift, scale, w_gate, w_up, layer, tm, tn=512):
    m, d = x.shape
    d_ff = w_gate.shape[-1]
    wspec = pl.BlockSpec((None, d, tn), lambda i, j: (layer, 0, j))
    return pl.pallas_call(
        _ffn_up_kernel,
        grid=(m // tm, d_ff // tn),
        in_specs=_row_specs(m, d, tm, shift) + [wspec, wspec],
        out_specs=pl.BlockSpec((tm, tn), lambda i, j: (i, j)),
        out_shape=jax.ShapeDtypeStruct((m, d_ff), BF16),
        scratch_shapes=[pltpu.VMEM((tm, d), BF16)],
        compiler_params=_cparams(("parallel", "arbitrary")),
        name="ffn_up",
    )(x, g_norm.reshape(1, d), shift, scale, w_gate, w_up)


def _mm_res_kernel(a_ref, w_ref, res_ref, gt_ref, o_ref):
    y = jnp.dot(a_ref[...], w_ref[...].astype(BF16), preferred_element_type=F32)
    o_ref[...] = res_ref[...] + gt_ref[...] * y


def _mm_res(a, w, layer, res, gate, tm):
    m, kdim = a.shape
    d = res.shape[-1]
    tn = 512 if kdim <= 2048 else 256
    groups = gate.shape[0]
    blocks_per_group = (m // groups) // tm
    return pl.pallas_call(
        _mm_res_kernel,
        grid=(m // tm, d // tn),
        in_specs=[
            pl.BlockSpec((tm, kdim), lambda i, j: (i, 0)),
            pl.BlockSpec((None, kdim, tn), lambda i, j: (layer, 0, j)),
            pl.BlockSpec((tm, tn), lambda i, j: (i, j)),
            pl.BlockSpec((None, gate.shape[1], tn), lambda i, j: (i // blocks_per_group, 0, j)),
        ],
        out_specs=pl.BlockSpec((tm, tn), lambda i, j: (i, j)),
        out_shape=jax.ShapeDtypeStruct((m, d), F32),
        compiler_params=_cparams(("parallel", "parallel")),
        name="mm_res",
    )(a, w, res, gate)


def _final_norm_kernel(x_ref, g_ref, o_ref):
    x = x_ref[...]
    ms = jnp.mean(x * x, axis=-1, keepdims=True)
    o_ref[...] = x * lax.rsqrt(ms + EPS) * g_ref[...]


def _final_norm(x, g_final, tm):
    m, d = x.shape
    return pl.pallas_call(
        _final_norm_kernel,
        grid=(m // tm,),
        in_specs=[pl.BlockSpec((tm, d), lambda i: (i, 0)), pl.BlockSpec((1, d), lambda i: (0, 0))],
        out_specs=pl.BlockSpec((tm, d), lambda i: (i, 0)),
        out_shape=jax.ShapeDtypeStruct((m, d), F32),
        compiler_params=_cparams(("parallel",)),
        name="final_norm",
    )(x, g_final.reshape(1, d))


def _branch_kernel(a_ref, m_ref, c_ref, ga_ref, gm_ref, gc_ref, wa_ref, wm_ref, wc_ref, o_ref):
    def term(x_ref, w_ref, g_ref):
        y = jnp.dot(x_ref[...], w_ref[...].astype(BF16), preferred_element_type=F32)
        return jax.nn.sigmoid(g_ref[...].astype(F32)) * y

    o_ref[...] = (term(a_ref, wa_ref, ga_ref) + term(m_ref, wm_ref, gm_ref)
                  + term(c_ref, wc_ref, gc_ref)).astype(o_ref.dtype)


def _branch_merge(a_out, hm, c_out, u_rest, w_br_a, w_br_m, w_br_c, layer, tm, tn=512):
    m = a_out.shape[0]
    d = w_br_a.shape[-1]
    gate0 = (3 * W_C) // tn
    per_gate = d // tn

    def gate_spec(g):
        return pl.BlockSpec((tm, tn), lambda i, j: (i, gate0 + g * per_gate + j))

    def w_spec(k):
        return pl.BlockSpec((None, k, tn), lambda i, j: (layer, 0, j))

    return pl.pallas_call(
        _branch_kernel,
        grid=(m // tm, d // tn),
        in_specs=[
            pl.BlockSpec((tm, A_W), lambda i, j: (i, 0)),
            pl.BlockSpec((tm, M_V), lambda i, j: (i, 0)),
            pl.BlockSpec((tm, W_C), lambda i, j: (i, 0)),
            gate_spec(0), gate_spec(1), gate_spec(2),
            w_spec(A_W), w_spec(M_V), w_spec(W_C),
        ],
        out_specs=pl.BlockSpec((tm, tn), lambda i, j: (i, j)),
        out_shape=jax.ShapeDtypeStruct((m, d), BF16),
        compiler_params=_cparams(("parallel", "parallel")),
        name="branch_merge",
    )(a_out, hm, c_out, u_rest, u_rest, u_rest, w_br_a, w_br_m, w_br_c)


ATTN_NB = 4


def _attn_blocks(blocks):
    qi = lax.broadcasted_iota(jnp.int32, (A_BLK, A_BLK), 0)
    ki = lax.broadcasted_iota(jnp.int32, (A_BLK, A_BLK), 1)
    dist = ki - qi
    scale = HD ** -0.5
    n = len(blocks)
    qb = [b[0].astype(BF16) for b in blocks]
    s_cur = [lax.dot_general(qb[i], blocks[i][1].astype(BF16), NT_DIMS, preferred_element_type=F32)
             for i in range(n)]
    s_prev = [lax.dot_general(qb[i], blocks[i][2].astype(BF16), NT_DIMS, preferred_element_type=F32)
              for i in range(n)]
    s_cur = [jnp.where(dist <= 0, s * scale, -jnp.inf) for s in s_cur]
    s_prev = [jnp.where(dist >= blocks[i][5], s_prev[i] * scale, -jnp.inf) for i in range(n)]
    mx = [jnp.maximum(jnp.max(s_cur[i], axis=-1, keepdims=True), jnp.max(s_prev[i], axis=-1, keepdims=True))
          for i in range(n)]
    p_cur = [jnp.exp(s_cur[i] - mx[i]) for i in range(n)]
    p_prev = [jnp.exp(s_prev[i] - mx[i]) for i in range(n)]
    den = [jnp.sum(p_cur[i], axis=-1, keepdims=True) + jnp.sum(p_prev[i], axis=-1, keepdims=True)
           for i in range(n)]
    o = [jnp.dot(p_cur[i].astype(BF16), blocks[i][3].astype(BF16), preferred_element_type=F32)
         + jnp.dot(p_prev[i].astype(BF16), blocks[i][4].astype(BF16), preferred_element_type=F32)
         for i in range(n)]
    return [(o[i] / den[i], mx[i] + jnp.log(den[i])) for i in range(n)]


def _attn_kernel(*refs):
    ins = refs[:5 * N_GROUPS]
    a_ref = refs[5 * N_GROUPS]
    o_scr = refs[5 * N_GROUPS + 1:5 * N_GROUPS + 1 + N_GROUPS]
    l_scr = refs[5 * N_GROUPS + 1 + N_GROUPS:]
    nb = ATTN_NB
    first_lim = jnp.where(pl.program_id(1) > 0, 0, A_BLK)
    for g in range(N_GROUPS):
        d = A_DILATIONS[g]
        q_ref, kc_ref, kp_ref, vc_ref, vp_ref = ins[5 * g:5 * g + 5]
        span = d * A_BLK
        nblk = SEG // span

        def rows(start, d=d):
            return pl.ds(start, A_BLK, stride=d) if d > 1 else pl.ds(start, A_BLK)

        def head(r, rows=rows, q_ref=q_ref, kc_ref=kc_ref, kp_ref=kp_ref, vc_ref=vc_ref, vp_ref=vp_ref):
            return (q_ref[rows(r), :], kc_ref[rows(r), :], kp_ref[rows(r), :],
                    vc_ref[rows(r), :], vp_ref[rows(r), :], first_lim)

        def inner(start, rows=rows, span=span, q_ref=q_ref, kc_ref=kc_ref, vc_ref=vc_ref):
            return (q_ref[rows(start), :], kc_ref[rows(start), :], kc_ref[rows(start - span), :],
                    vc_ref[rows(start), :], vc_ref[rows(start - span), :], 0)

        def run(starts, blocks, g=g, rows=rows):
            for start, (o, lse) in zip(starts, _attn_blocks(blocks)):
                o_scr[g][rows(start), :] = o
                l_scr[g][rows(start), :] = jnp.broadcast_to(lse, (A_BLK, LANES))

        if nblk == 1:
            def body(t, c, run=run, head=head):
                rs = [t * nb + i for i in range(nb)]
                run(rs, [head(r) for r in rs])
                return c
            lax.fori_loop(0, d // nb, body, 0)
        else:
            assert nblk % nb == 0

            def body(r, c, run=run, head=head, inner=inner, span=span, nblk=nblk):
                starts = [r + i * span for i in range(nb)]
                run(starts, [head(r)] + [inner(s) for s in starts[1:]])

                def more(t, c2):
                    st = [r + (t * nb + i) * span for i in range(nb)]
                    run(st, [inner(s) for s in st])
                    return c2
                if nblk > nb:
                    lax.fori_loop(1, nblk // nb, more, 0)
                return c
            if d == 1:
                body(0, 0)
            else:
                lax.fori_loop(0, d, body, 0)

    chunk = 64

    def merge(c, carry):
        sl = pl.ds(pl.multiple_of(c * chunk, chunk), chunk)
        l0, l1, l2 = l_scr[0][sl, :], l_scr[1][sl, :], l_scr[2][sl, :]
        mx = jnp.maximum(jnp.maximum(l0, l1), l2)
        e0, e1, e2 = jnp.exp(l0 - mx), jnp.exp(l1 - mx), jnp.exp(l2 - mx)
        den = e0 + e1 + e2
        a = (e0 / den) * o_scr[0][sl, :] + (e1 / den) * o_scr[1][sl, :] + (e2 / den) * o_scr[2][sl, :]
        a_ref[sl, :] = a.astype(a_ref.dtype)
        return carry
    lax.fori_loop(0, SEG // chunk, merge, 0)


def _attn_prompt(u_attn, batch, seq):
    assert seq % SEG == 0
    nseg = seq // SEG
    specs = []
    for g in range(N_GROUPS):
        d = A_DILATIONS[g]
        assert A_WINDOWS[g] // d == A_BLK
        per_seg = SEG // (d * A_BLK)

        def cur(part, g=g):
            return pl.BlockSpec((SEG, HD), lambda b, s, h: (b * nseg + s, (3 * g + part) * N_HEADS + h))

        def prev(part, g=g, d=d, per_seg=per_seg):
            return pl.BlockSpec((d * A_BLK, HD),
                                lambda b, s, h: (jnp.maximum((b * nseg + s) * per_seg - 1, 0),
                                                 (3 * g + part) * N_HEADS + h))
        specs += [cur(0), cur(1), prev(1), cur(2), prev(2)]
    return pl.pallas_call(
        _attn_kernel,
        grid=(batch, nseg, N_HEADS),
        in_specs=specs,
        out_specs=pl.BlockSpec((SEG, HD), lambda b, s, h: (b * nseg + s, h)),
        out_shape=jax.ShapeDtypeStruct((batch * seq, A_W), BF16),
        scratch_shapes=[pltpu.VMEM((SEG, HD), F32)] * (2 * N_GROUPS),
        compiler_params=_cparams(("parallel", "parallel", "parallel")),
        name="attn_prompt",
    )(*([u_attn] * (5 * N_GROUPS)))


def _conv_prompt_kernel(blocks_per_seq, u_ref, b_ref, c_ref, cw_ref, co_ref, st_ref, p_scr):
    tm = u_ref.shape[0]
    i = pl.program_id(0)

    @pl.when(i % blocks_per_seq == 0)
    def _():
        p_scr[0:8, :] = jnp.zeros((8, p_scr.shape[1]), F32)

    @pl.when(i % blocks_per_seq != 0)
    def _():
        p_scr[0:8, :] = p_scr[tm:tm + 8, :]

    p = c_ref[...].astype(F32) * u_ref[...].astype(F32)
    p_scr[8:8 + tm, :] = p
    z = (cw_ref[0:1, :] * p_scr[6:6 + tm, :] + cw_ref[1:2, :] * p_scr[7:7 + tm, :]
         + cw_ref[2:3, :] * p)
    co_ref[...] = (b_ref[...].astype(F32) * z).astype(co_ref.dtype)
    st_ref[...] = p[tm - (CONV_W - 1):tm, :]


def _conv_prompt(u_rest, conv_w, batch, seq, tm=512):
    m = batch * seq
    blocks_per_seq = seq // tm
    return pl.pallas_call(
        functools.partial(_conv_prompt_kernel, blocks_per_seq),
        grid=(m // tm,),
        in_specs=[
            pl.BlockSpec((tm, W_C), lambda i: (i, 0)),
            pl.BlockSpec((tm, W_C), lambda i: (i, 1)),
            pl.BlockSpec((tm, W_C), lambda i: (i, 2)),
            pl.BlockSpec((CONV_W, W_C), lambda i: (0, 0)),
        ],
        out_specs=[
            pl.BlockSpec((tm, W_C), lambda i: (i, 0)),
            pl.BlockSpec((None, CONV_W - 1, W_C), lambda i: (i // blocks_per_seq, 0, 0)),
        ],
        out_shape=[
            jax.ShapeDtypeStruct((m, W_C), BF16),
            jax.ShapeDtypeStruct((batch, CONV_W - 1, W_C), F32),
        ],
        scratch_shapes=[pltpu.VMEM((tm + 8, W_C), F32)],
        compiler_params=_cparams(("arbitrary",)),
        name="conv_prompt",
    )(u_rest, u_rest, u_rest, conv_w)


M_CHUNK = 128


def _log_sigmoid(x):
    return jnp.minimum(x, 0.0) - jnp.log1p(jnp.exp(-jnp.abs(x)))


def _mlstm_kernel(ua_ref, ub_ref, gt_ref, bias_ref, gm_ref, hm_ref, c_ref, n_ref, m_ref):
    L = M_CHUNK

    @pl.when(pl.program_id(1) == 0)
    def _():
        c_ref[...] = jnp.zeros_like(c_ref)
        n_ref[...] = jnp.zeros_like(n_ref)
        m_ref[...] = jnp.zeros_like(m_ref)

    lane = lax.broadcasted_iota(jnp.int32, (L, LANES), 1)
    row = lax.broadcasted_iota(jnp.int32, (L, L), 0)
    col = lax.broadcasted_iota(jnp.int32, (L, L), 1)
    causal = col <= row
    scale = HD ** -0.5

    g = gt_ref[...] + bias_ref[...]
    x = jnp.where(lane < N_HEADS, g, _log_sigmoid(g))
    cum = jnp.dot(causal.astype(F32), x, precision=lax.Precision.HIGHEST,
                  preferred_element_type=F32)
    x_t = x.T
    cum_t = cum.T
    m_all = m_ref[...]
    m_tile = jnp.zeros((1, LANES), F32)
    lane1 = lax.broadcasted_iota(jnp.int32, (1, LANES), 1)
    tn = (((0,), (0,)), ((), ()))
    for h in range(N_HEADS):
        ig_row = x_t[h:h + 1, :]
        ig_col = x[:, h:h + 1]
        b_row = cum_t[N_HEADS + h:N_HEADS + h + 1, :]
        b_col = cum[:, N_HEADS + h:N_HEADS + h + 1]
        m_st = m_all[:, h:h + 1]
        dlog = jnp.where(causal, b_col - b_row + ig_row, -jnp.inf)
        g_col = b_col + m_st
        m_row = jnp.maximum(g_col, jnp.max(dlog, axis=-1, keepdims=True))
        w_intra = jnp.exp(dlog - m_row)
        w_inter = jnp.exp(g_col - m_row)
        q = ua_ref[:, h * HD:(h + 1) * HD]
        k = ua_ref[:, M_QK + h * HD:M_QK + (h + 1) * HD]
        if h < 2:
            v = ua_ref[:, 2 * M_QK + h * DV:2 * M_QK + (h + 1) * DV]
        else:
            v = ub_ref[:, (h - 2) * DV:(h - 1) * DV]
        og = ub_ref[:, 2 * DV + h * DV:2 * DV + (h + 1) * DV]
        c_st = c_ref[h]
        n_st = n_ref[h:h + 1, :]
        s = lax.dot_general(q, k, NT_DIMS, preferred_element_type=F32) * (w_intra * scale)
        num = (jnp.dot(s.astype(BF16), v, preferred_element_type=F32)
               + w_inter * jnp.dot(q, c_st.astype(BF16), preferred_element_type=F32))
        den = (jnp.sum(s, axis=-1, keepdims=True)
               + w_inter * jnp.sum(q.astype(F32) * n_st, axis=-1, keepdims=True))
        hh = num / jnp.maximum(jnp.abs(den), jnp.exp(-m_row))
        hn = hh * lax.rsqrt(jnp.mean(hh * hh, axis=-1, keepdims=True) + EPS) * gm_ref[:, h * DV:(h + 1) * DV]
        hm_ref[:, h * DV:(h + 1) * DV] = (hn * jax.nn.sigmoid(og.astype(F32))).astype(hm_ref.dtype)
        b_last = b_col[L - 1:L, :]
        g_last = b_last + m_st
        wlog = b_last - b_col + ig_col
        m_new = jnp.maximum(g_last, jnp.max(wlog, axis=0, keepdims=True))
        w_s = jnp.exp(wlog - m_new) * scale
        decay = jnp.exp(g_last - m_new)
        kv = lax.dot_general(k, (w_s * v.astype(F32)).astype(BF16), tn, preferred_element_type=F32)
        c_ref[h] = decay * c_st + kv
        n_ref[h:h + 1, :] = decay * n_st + jnp.sum(w_s * k.astype(F32), axis=0, keepdims=True)
        m_tile = jnp.where(lane1 == h, m_new, m_tile)
    m_ref[...] = m_tile


def _mlstm_prompt(u_m, gates, gate_bias, g_mlstm, batch, seq):
    m = batch * seq
    L = M_CHUNK
    nc = seq // L
    half = U_MLSTM // 2
    return pl.pallas_call(
        _mlstm_kernel,
        grid=(batch, nc),
        in_specs=[
            pl.BlockSpec((L, half), lambda b, c: (b * nc + c, 0)),
            pl.BlockSpec((L, half), lambda b, c: (b * nc + c, 1)),
            pl.BlockSpec((L, LANES), lambda b, c: (b * nc + c, 0)),
            pl.BlockSpec((1, LANES), lambda b, c: (0, 0)),
            pl.BlockSpec((1, M_V), lambda b, c: (0, 0)),
        ],
        out_specs=[
            pl.BlockSpec((L, M_V), lambda b, c: (b * nc + c, 0)),
            pl.BlockSpec((None, N_HEADS, HD, DV), lambda b, c: (b, 0, 0, 0)),
            pl.BlockSpec((None, N_HEADS, HD), lambda b, c: (b, 0, 0)),
            pl.BlockSpec((None, 1, LANES), lambda b, c: (b, 0, 0)),
        ],
        out_shape=[
            jax.ShapeDtypeStruct((m, M_V), BF16),
            jax.ShapeDtypeStruct((batch, N_HEADS, HD, DV), F32),
            jax.ShapeDtypeStruct((batch, N_HEADS, HD), F32),
            jax.ShapeDtypeStruct((batch, 1, LANES), F32),
        ],
        compiler_params=_cparams(("parallel", "arbitrary")),
        name="mlstm_prompt",
    )(u_m, u_m, gates, gate_bias, g_mlstm.reshape(1, M_V))


def _mid_sample_kernel(qkv_ref, r_ref, k1_ref, k2_ref, k3_ref, st_ref, cw_ref, a_ref, co_ref, ns_ref):
    scale = HD ** -0.5
    caches = (k1_ref, k2_ref, k3_ref)
    outs, lses = [], []
    for g in range(N_GROUPS):
        base = 3 * g * N_HEADS
        q = qkv_ref[base:base + N_HEADS, :]
        k_new = qkv_ref[base + N_HEADS:base + 2 * N_HEADS, :]
        v_new = qkv_ref[base + 2 * N_HEADS:base + 3 * N_HEADS, :]
        k_c = caches[g][:, 0]
        v_c = caches[g][:, 1]
        s_c = jnp.sum(k_c * q[None], axis=-1, keepdims=True) * scale
        s_n = jnp.sum(k_new * q, axis=-1, keepdims=True) * scale
        mx = jnp.maximum(jnp.max(s_c, axis=0), s_n)
        p_c = jnp.exp(s_c - mx[None])
        p_n = jnp.exp(s_n - mx)
        den = jnp.sum(p_c, axis=0) + p_n
        outs.append((jnp.sum(p_c * v_c, axis=0) + p_n * v_new) / den)
        lses.append(mx + jnp.log(den))
    mx = jnp.maximum(jnp.maximum(lses[0], lses[1]), lses[2])
    e = [jnp.exp(l - mx) for l in lses]
    den = e[0] + e[1] + e[2]
    a = (e[0] / den) * outs[0] + (e[1] / den) * outs[1] + (e[2] / den) * outs[2]
    a_ref[...] = a.astype(a_ref.dtype)
    p = r_ref[:, 2 * W_C:3 * W_C].astype(F32) * r_ref[:, 0:W_C].astype(F32)
    z = cw_ref[0:1, :] * st_ref[0:1, :] + cw_ref[1:2, :] * st_ref[1:2, :] + cw_ref[2:3, :] * p
    co_ref[...] = (r_ref[:, W_C:2 * W_C].astype(F32) * z).astype(co_ref.dtype)
    ns_ref[0:1, :] = st_ref[1:2, :]
    ns_ref[1:2, :] = p


def _mid_sample(u_attn, u_rest, caches, state_conv, conv_w, layer):
    nb = u_attn.shape[0]
    views, cache_specs = [], []
    for g in range(N_GROUPS):
        dil = A_DILATIONS[g]
        c = caches[g]
        assert c.shape[2] == A_WINDOWS[g]
        views.append(c.reshape(c.shape[0], nb, A_BLK, dil, 2, N_HEADS, HD))
        cache_specs.append(pl.BlockSpec((None, None, A_BLK, None, 2, N_HEADS, HD),
                                        lambda b: (layer, b, 0, 0, 0, 0, 0)))
    return pl.pallas_call(
        _mid_sample_kernel,
        grid=(nb,),
        in_specs=[
            pl.BlockSpec((None, 9 * N_HEADS, HD), lambda b: (b, 0, 0)),
            pl.BlockSpec((None, 1, 3 * W_C), lambda b: (b, 0, 0)),
        ] + cache_specs + [
            pl.BlockSpec((None, None, CONV_W - 1, W_C), lambda b: (layer, b, 0, 0)),
            pl.BlockSpec((CONV_W, W_C), lambda b: (0, 0)),
        ],
        out_specs=[
            pl.BlockSpec((None, N_HEADS, HD), lambda b: (b, 0, 0)),
            pl.BlockSpec((None, 1, W_C), lambda b: (b, 0, 0)),
            pl.BlockSpec((None, CONV_W - 1, W_C), lambda b: (b, 0, 0)),
        ],
        out_shape=[
            jax.ShapeDtypeStruct((nb, N_HEADS, HD), BF16),
            jax.ShapeDtypeStruct((nb, 1, W_C), BF16),
            jax.ShapeDtypeStruct((nb, CONV_W - 1, W_C), F32),
        ],
        compiler_params=_cparams(("parallel",)),
        name="mid_sample",
    )(u_attn.reshape(nb, 9 * N_HEADS, HD), u_rest.reshape(nb, 1, -1), *views, state_conv, conv_w)


def _mlstm_step_kernel(u_ref, gt_ref, bias_ref, gm_ref, c_ref, n_ref, m_ref,
                       hm_ref, co_ref, no_ref, mo_ref):
    scale = HD ** -0.5
    g = gt_ref[...] + bias_ref[...]
    lf_all = _log_sigmoid(g)
    m_all = m_ref[...]
    lane1 = lax.broadcasted_iota(jnp.int32, (1, LANES), 1)
    eye = (lax.broadcasted_iota(jnp.int32, (HD, HD), 0)
           == lax.broadcasted_iota(jnp.int32, (HD, HD), 1))
    m_tile = jnp.zeros((1, LANES), F32)
    for h in range(N_HEADS):
        ig = g[:, h:h + 1]
        lf = lf_all[:, N_HEADS + h:N_HEADS + h + 1]
        m_st = m_all[:, h:h + 1]
        q = u_ref[:, h * HD:(h + 1) * HD]
        k = u_ref[:, M_QK + h * HD:M_QK + (h + 1) * HD]
        v = u_ref[:, 2 * M_QK + h * DV:2 * M_QK + (h + 1) * DV]
        og = u_ref[:, 2 * M_QK + M_V + h * DV:2 * M_QK + M_V + (h + 1) * DV]
        qf, kf, vf = q.astype(F32), k.astype(F32), v.astype(F32)
        c_st = c_ref[h]
        n_st = n_ref[h:h + 1, :]
        g_col = lf + m_st
        m_row = jnp.maximum(g_col, ig)
        w_intra = jnp.exp(ig - m_row)
        w_inter = jnp.exp(g_col - m_row)
        s = jnp.sum(qf * kf, axis=-1, keepdims=True) * (w_intra * scale)
        q_rows = jnp.broadcast_to(qf, (16, HD)).astype(BF16)
        qc = jnp.dot(q_rows, c_st.astype(BF16), preferred_element_type=F32)[0:1, :]
        num = s * vf + w_inter * qc
        den = s + w_inter * jnp.sum(qf * n_st, axis=-1, keepdims=True)
        hh = num / jnp.maximum(jnp.abs(den), jnp.exp(-m_row))
        hn = hh * lax.rsqrt(jnp.mean(hh * hh, axis=-1, keepdims=True) + EPS) * gm_ref[:, h * DV:(h + 1) * DV]
        hm_ref[:, h * DV:(h + 1) * DV] = (hn * jax.nn.sigmoid(og.astype(F32))).astype(hm_ref.dtype)
        w_s = w_intra * scale
        k_diag = jnp.where(eye, jnp.broadcast_to(kf, (HD, HD)), 0.0).astype(BF16)
        v_rows = jnp.broadcast_to(w_s * vf, (HD, DV)).astype(BF16)
        kv = jnp.dot(k_diag, v_rows, preferred_element_type=F32)
        co_ref[h] = w_inter * c_st + kv
        no_ref[h:h + 1, :] = w_inter * n_st + w_s * kf
        m_tile = jnp.where(lane1 == h, m_row, m_tile)
    mo_ref[...] = m_tile


def _mlstm_step(u_m, gates, gate_bias, g_mlstm, state_c, state_n, m_pad, layer):
    nb = u_m.shape[0]
    return pl.pallas_call(
        _mlstm_step_kernel,
        grid=(nb,),
        in_specs=[
            pl.BlockSpec((None, 1, U_MLSTM), lambda b: (b, 0, 0)),
            pl.BlockSpec((None, 1, LANES), lambda b: (b, 0, 0)),
            pl.BlockSpec((1, LANES), lambda b: (0, 0)),
            pl.BlockSpec((1, M_V), lambda b: (0, 0)),
            pl.BlockSpec((None, None, N_HEADS, HD, DV), lambda b: (layer, b, 0, 0, 0)),
            pl.BlockSpec((None, None, N_HEADS, HD), lambda b: (layer, b, 0, 0)),
            pl.BlockSpec((None, 1, LANES), lambda b: (b, 0, 0)),
        ],
        out_specs=[
            pl.BlockSpec((None, 1, M_V), lambda b: (b, 0, 0)),
            pl.BlockSpec((None, N_HEADS, HD, DV), lambda b: (b, 0, 0, 0)),
            pl.BlockSpec((None, N_HEADS, HD), lambda b: (b, 0, 0)),
            pl.BlockSpec((None, 1, LANES), lambda b: (b, 0, 0)),
        ],
        out_shape=[
            jax.ShapeDtypeStruct((nb, 1, M_V), BF16),
            jax.ShapeDtypeStruct((nb, N_HEADS, HD, DV), F32),
            jax.ShapeDtypeStruct((nb, N_HEADS, HD), F32),
            jax.ShapeDtypeStruct((nb, 1, LANES), F32),
        ],
        compiler_params=_cparams(("parallel",)),
        name="mlstm_step",
    )(u_m.reshape(nb, 1, U_MLSTM), gates.reshape(nb, 1, LANES), gate_bias,
      g_mlstm.reshape(1, M_V), state_c, state_n, m_pad)


def _pad_lanes(t):
    return jnp.pad(t, [(0, 0)] * (t.ndim - 1) + [(0, LANES - t.shape[-1])])


def _kv_rows(u_attn, group, lead_shape):
    k = u_attn[..., (3 * group + 1) * A_W:(3 * group + 2) * A_W]
    v = u_attn[..., (3 * group + 2) * A_W:(3 * group + 3) * A_W]
    return jnp.stack([k, v], axis=-2).reshape(lead_shape + (2, N_HEADS, HD))


def _run_group(x, mods, states, p, prompt):
    batch, seq, d = x.shape
    m = batch * seq
    depth = p["w_in_t"].shape[0]
    tm = 1024 if prompt else m
    xf = x.reshape(m, d)
    new_states = []
    for l in range(depth):
        sh1, sc1, gt1, sh2, sc2, gt2 = mods[l]
        u_attn, u_m, gates, u_rest = _in_proj(xf, p["g_norm1"][l], sh1, sc1, p["w_in_t"], p["w_in_bf"], l, tm)
        gate_bias = p["gate_bias"][l]
        if prompt:
            a_out = _attn_prompt(u_attn, batch, seq)
            c_out, new_conv = _conv_prompt(u_rest, p["conv_w"][l], batch, seq)
            hm, c_n, n_n, m_n = _mlstm_prompt(u_m, gates, gate_bias, p["g_mlstm"][l], batch, seq)
            u3 = u_attn.reshape(batch, seq, U_ATTN)
            bufs = []
            for g in range(N_GROUPS):
                keep = min(A_WINDOWS[g], seq)
                bufs.append(_kv_rows(u3[:, seq - keep:], g, (batch, keep)))
        else:
            caches, st_c, st_n, st_m, st_conv = states
            a_out, c_out, new_conv = _mid_sample(u_attn, u_rest[:, :3 * W_C], caches, st_conv,
                                                 p["conv_w"][l], l)
            a_out = a_out.reshape(m, A_W)
            c_out = c_out.reshape(m, W_C)
            hm, c_n, n_n, m_n = _mlstm_step(u_m, gates, gate_bias, p["g_mlstm"][l],
                                            st_c, st_n, _pad_lanes(st_m[l])[:, None, :], l)
            hm = hm.reshape(m, M_V)
            bufs = [_kv_rows(u_attn, g, (batch, seq)) for g in range(N_GROUPS)]
        merged = _branch_merge(a_out, hm, c_out, u_rest, p["w_br_a"], p["w_br_m"], p["w_br_c"], l, tm)
        xf = _mm_res(merged, p["w_mix_out"], l, xf, gt1, tm)
        act = _ffn_up(xf, p["g_norm2"][l], sh2, sc2, p["w_ffn_gate"], p["w_ffn_up"], l, tm)
        xf = _mm_res(act, p["w_ffn_down"], l, xf, gt2, tm)
        new_states.append(bufs + [c_n, n_n, m_n[:, 0, :N_HEADS], new_conv])
    y = _final_norm(xf, p["g_final"], min(m, 512))
    stacked = [jnp.stack([n[i] for n in new_states], axis=0) for i in range(7)]
    return y.reshape(batch, seq, d), stacked


def kernel(x_prompt, x_sample, cache_a1_kv, cache_a2_kv, cache_a3_kv, state_mlstm_c, state_mlstm_n,
           state_mlstm_m, state_conv, c_prompt, c_sample, w_ada, b_ada, g_norm1, g_norm2, w_in, b_igate,
           b_fgate, g_mlstm, conv_w, w_br_a, w_br_m, w_br_c, w_mix_out, w_ffn_gate, w_ffn_up, w_ffn_down,
           g_final):
    depth, d = g_norm1.shape
    nbp = c_prompt.shape[0]
    nbs = c_sample.shape[0]
    assert w_in.shape[-1] == U_REST_OFF + 3 * W_C + 3 * d

    c_all = jnp.concatenate([c_prompt, c_sample], axis=0)
    c_all = jnp.pad(c_all, ((0, (-c_all.shape[0]) % 8), (0, 0)))
    ada = _ada(c_all, w_ada, b_ada)
    mods_p, mods_s = [], []
    for l in range(depth):
        mods_p.append([ada[l, :nbp, i * d:(i + 1) * d].reshape(nbp, 1, d) for i in range(6)])
        mods_s.append([ada[l, nbp:nbp + nbs, i * d:(i + 1) * d].reshape(1, nbs, d) for i in range(6)])

    w_in_t = jnp.swapaxes(w_in, 1, 2)
    params = dict(
        w_in_bf=_w_in_cast(w_in_t),
        g_norm1=g_norm1, g_norm2=g_norm2, g_mlstm=g_mlstm, conv_w=conv_w,
        w_br_a=w_br_a, w_br_m=w_br_m, w_br_c=w_br_c, w_mix_out=w_mix_out,
        w_ffn_gate=w_ffn_gate, w_ffn_up=w_ffn_up, w_ffn_down=w_ffn_down, g_final=g_final,
        w_in_t=w_in_t,
        gate_bias=_pad_lanes(jnp.concatenate([b_igate, b_fgate], axis=-1))[:, None, :],
    )

    y_p, ps = _run_group(x_prompt, mods_p, None, params, True)
    caches = (cache_a1_kv, cache_a2_kv, cache_a3_kv)
    states = (caches, state_mlstm_c, state_mlstm_n, state_mlstm_m, state_conv)
    y_s, ss = _run_group(x_sample, mods_s, states, params, False)
    seq_s = x_sample.shape[1]
    for g in range(N_GROUPS):
        win = caches[g].shape[2]
        keep = [(0, 0, 0)] * 2
        tail = [(0, 0, 0)] * 3
        zero = jnp.zeros((), F32)
        shifted = lax.pad(caches[g], zero, keep + [(-seq_s, seq_s, 0)] + tail)
        ss[g] = shifted + lax.pad(ss[g], zero, keep + [(win - seq_s, 0, 0)] + tail)
    return (y_p, y_s, ps[0], ss[0], ps[1], ss[1], ps[2], ss[2], ps[3], ss[3], ps[4], ss[4],
            ps[5], ss[5], ps[6], ss[6])
```

```python
import functools

import jax
import jax.numpy as jnp
from jax import lax
from jax.experimental import pallas as pl
from jax.experimental.pallas import tpu as pltpu

F32 = jnp.float32
BF16 = jnp.bfloat16
EPS = 1e-6

A_WINDOWS = (128, 512, 2048)
A_DILATIONS = (1, 4, 16)
N_GROUPS = 3
HD = 128
A_BLK = 128
N_HEADS = 4
DV = 256
A_W = N_HEADS * HD
M_QK = N_HEADS * HD
M_V = N_HEADS * DV
W_C = 1024
CONV_W = 3
U_ATTN = 9 * A_W
U_MLSTM = 2 * M_QK + 2 * M_V
U_MAIN = U_ATTN + U_MLSTM
N_GATE_COLS = 2 * N_HEADS
U_REST_OFF = U_MAIN + N_GATE_COLS
LANES = 128
SUBLANES = 8
SEG = A_DILATIONS[-1] * A_BLK

VMEM_LIMIT = 52 * 1024 * 1024
NT_DIMS = (((1,), (1,)), ((), ()))


def _cparams(sem):
    return pltpu.CompilerParams(dimension_semantics=sem, vmem_limit_bytes=VMEM_LIMIT)


def _ada_kernel(c_ref, w_ref, b_ref, o_ref):
    c = c_ref[...]
    a = (c * jax.nn.sigmoid(c)).astype(BF16)
    o_ref[...] = jnp.dot(a, w_ref[...].astype(BF16), preferred_element_type=F32) + b_ref[...]


def _ada(c_all, w_ada, b_ada, tn=1024):
    depth, d, n = w_ada.shape
    r = c_all.shape[0]
    return pl.pallas_call(
        _ada_kernel,
        grid=(depth, n // tn),
        in_specs=[
            pl.BlockSpec((r, d), lambda l, j: (0, 0)),
            pl.BlockSpec((None, d, tn), lambda l, j: (l, 0, j)),
            pl.BlockSpec((None, 1, tn), lambda l, j: (l, 0, j)),
        ],
        out_specs=pl.BlockSpec((None, r, tn), lambda l, j: (l, 0, j)),
        out_shape=jax.ShapeDtypeStruct((depth, r, n), F32),
        compiler_params=_cparams(("parallel", "parallel")),
        name="ada",
    )(c_all, w_ada, b_ada.reshape(depth, 1, n))


def _modulated_norm(x_ref, g_ref, sh_ref, sc_ref, h_scr):
    x = x_ref[...]
    ms = jnp.mean(x * x, axis=-1, keepdims=True)
    y = x * lax.rsqrt(ms + EPS) * g_ref[...]
    h_scr[...] = (y * (1.0 + sc_ref[...]) + sh_ref[...]).astype(BF16)


def _row_specs(m, d, tm, shift):
    groups = shift.shape[0]
    blocks_per_group = (m // groups) // tm
    mod = pl.BlockSpec((None, shift.shape[1], d), lambda i, j: (i // blocks_per_group, 0, 0))
    return [
        pl.BlockSpec((tm, d), lambda i, j: (i, 0)),
        pl.BlockSpec((1, d), lambda i, j: (0, 0)),
        mod,
        mod,
    ]


IN_TN = 512
ATTN_TILES = U_ATTN // IN_TN
MAIN_TILES = U_MAIN // IN_TN


def _in_proj_kernel(x_ref, g_ref, sh_ref, sc_ref, w_ref, wg_ref, wn_ref,
                    oa_ref, om_ref, og_ref, or_ref, h_scr):
    j = pl.program_id(1)

    @pl.when(j == 0)
    def _():
        _modulated_norm(x_ref, g_ref, sh_ref, sc_ref, h_scr)
        wg = jnp.concatenate([wg_ref[...], jnp.zeros((LANES - N_GATE_COLS, wg_ref.shape[1]), F32)], axis=0)
        og_ref[...] = lax.dot_general(h_scr[...], wg.astype(BF16), NT_DIMS, preferred_element_type=F32)

    @pl.when(j < ATTN_TILES)
    def _():
        oa_ref[...] = lax.dot_general(h_scr[...], w_ref[...].astype(BF16), NT_DIMS,
                                      preferred_element_type=F32)

    @pl.when(jnp.logical_and(j >= ATTN_TILES, j < MAIN_TILES))
    def _():
        om_ref[...] = lax.dot_general(h_scr[...], w_ref[...].astype(BF16), NT_DIMS,
                                      preferred_element_type=F32).astype(om_ref.dtype)

    @pl.when(j >= MAIN_TILES)
    def _():
        w = jnp.concatenate([w_ref[N_GATE_COLS:, :], wn_ref[...]], axis=0).astype(BF16)
        or_ref[...] = lax.dot_general(h_scr[...], w, NT_DIMS,
                                      preferred_element_type=F32).astype(or_ref.dtype)


def _in_proj(x, g_norm, shift, scale, w_t, layer, tm):
    m, d = x.shape
    tn = IN_TN
    n_rest = w_t.shape[1] - U_REST_OFF
    n_tiles = MAIN_TILES + n_rest // tn
    assert N_GATE_COLS == SUBLANES and n_rest % tn == 0
    rows_per_tile = tn // SUBLANES
    return pl.pallas_call(
        _in_proj_kernel,
        grid=(m // tm, n_tiles),
        in_specs=_row_specs(m, d, tm, shift) + [
            pl.BlockSpec((None, tn, d), lambda i, j: (layer, j, 0)),
            pl.BlockSpec((None, SUBLANES, d), lambda i, j: (layer, U_MAIN // SUBLANES, 0)),
            pl.BlockSpec((None, SUBLANES, d),
                         lambda i, j: (layer, (jnp.maximum(j, MAIN_TILES) + 1) * rows_per_tile, 0)),
        ],
        out_specs=[
            pl.BlockSpec((tm, tn), lambda i, j: (i, jnp.minimum(j, ATTN_TILES - 1))),
            pl.BlockSpec((tm, tn), lambda i, j: (i, jnp.clip(j - ATTN_TILES, 0, MAIN_TILES - ATTN_TILES - 1))),
            pl.BlockSpec((tm, LANES), lambda i, j: (i, 0)),
            pl.BlockSpec((tm, tn), lambda i, j: (i, jnp.maximum(j - MAIN_TILES, 0))),
        ],
        out_shape=[
            jax.ShapeDtypeStruct((m, U_ATTN), F32),
            jax.ShapeDtypeStruct((m, U_MLSTM), BF16),
            jax.ShapeDtypeStruct((m, LANES), F32),
            jax.ShapeDtypeStruct((m, n_rest), BF16),
        ],
        scratch_shapes=[pltpu.VMEM((tm, d), BF16)],
        compiler_params=_cparams(("parallel", "arbitrary")),
        name="in_proj",
    )(x, g_norm.reshape(1, d), shift, scale, w_t, w_t, w_t)


def _ffn_up_kernel(x_ref, g_ref, sh_ref, sc_ref, wg_ref, wu_ref, o_ref, h_scr):
    @pl.when(pl.program_id(1) == 0)
    def _():
        _modulated_norm(x_ref, g_ref, sh_ref, sc_ref, h_scr)

    h = h_scr[...]
    a = jnp.dot(h, wg_ref[...].astype(BF16), preferred_element_type=F32)
    b = jnp.dot(h, wu_ref[...].astype(BF16), preferred_element_type=F32)
    o_ref[...] = (a * jax.nn.sigmoid(a) * b).astype(o_ref.dtype)


def _ffn_up(x, g_norm, shift, scale, w_gate, w_up, layer, tm, tn=512):
    m, d = x.shape
    d_ff = w_gate.shape[-1]
    wspec = pl.BlockSpec((None, d, tn), lambda i, j: (layer, 0, j))
    return pl.pallas_call(
        _ffn_up_kernel,
        grid=(m // tm, d_ff // tn),
        in_specs=_row_specs(m, d, tm, shift) + [wspec, wspec],
        out_specs=pl.BlockSpec((tm, tn), lambda i, j: (i, j)),
        out_shape=jax.ShapeDtypeStruct((m, d_ff), BF16),
        scratch_shapes=[pltpu.VMEM((tm, d), BF16)],
        compiler_params=_cparams(("parallel", "arbitrary")),
        name="ffn_up",
    )(x, g_norm.reshape(1, d), shift, scale, w_gate, w_up)


def _mm_res_kernel(a_ref, w_ref, res_ref, gt_ref, o_ref):
    y = jnp.dot(a_ref[...], w_ref[...].astype(BF16), preferred_element_type=F32)
    o_ref[...] = res_ref[...] + gt_ref[...] * y


def _mm_res(a, w, layer, res, gate, tm):
    m, kdim = a.shape
    d = res.shape[-1]
    tn = 512 if kdim <= 2048 else 256
    groups = gate.shape[0]
    blocks_per_group = (m // groups) // tm
    return pl.pallas_call(
        _mm_res_kernel,
        grid=(m // tm, d // tn),
        in_specs=[
            pl.BlockSpec((tm, kdim), lambda i, j: (i, 0)),
            pl.BlockSpec((None, kdim, tn), lambda i, j: (layer, 0, j)),
            pl.BlockSpec((tm, tn), lambda i, j: (i, j)),
            pl.BlockSpec((None, gate.shape[1], tn), lambda i, j: (i // blocks_per_group, 0, j)),
        ],
        out_specs=pl.BlockSpec((tm, tn), lambda i, j: (i, j)),
        out_shape=jax.ShapeDtypeStruct((m, d), F32),
        compiler_params=_cparams(("parallel", "parallel")),
        name="mm_res",
    )(a, w, res, gate)


def _final_norm_kernel(x_ref, g_ref, o_ref):
    x = x_ref[...]
    ms = jnp.mean(x * x, axis=-1, keepdims=True)
    o_ref[...] = x * lax.rsqrt(ms + EPS) * g_ref[...]


def _final_norm(x, g_final, tm):
    m, d = x.shape
    return pl.pallas_call(
        _final_norm_kernel,
        grid=(m // tm,),
        in_specs=[pl.BlockSpec((tm, d), lambda i: (i, 0)), pl.BlockSpec((1, d), lambda i: (0, 0))],
        out_specs=pl.BlockSpec((tm, d), lambda i: (i, 0)),
        out_shape=jax.ShapeDtypeStruct((m, d), F32),
        compiler_params=_cparams(("parallel",)),
        name="final_norm",
    )(x, g_final.reshape(1, d))


def _branch_kernel(a_ref, m_ref, c_ref, ga_ref, gm_ref, gc_ref, wa_ref, wm_ref, wc_ref, o_ref):
    def term(x_ref, w_ref, g_ref):
        y = jnp.dot(x_ref[...], w_ref[...].astype(BF16), preferred_element_type=F32)
        return jax.nn.sigmoid(g_ref[...].astype(F32)) * y

    o_ref[...] = (term(a_ref, wa_ref, ga_ref) + term(m_ref, wm_ref, gm_ref)
                  + term(c_ref, wc_ref, gc_ref)).astype(o_ref.dtype)


def _branch_merge(a_out, hm, c_out, u_rest, w_br_a, w_br_m, w_br_c, layer, tm, tn=512):
    m = a_out.shape[0]
    d = w_br_a.shape[-1]
    gate0 = (3 * W_C) // tn
    per_gate = d // tn

    def gate_spec(g):
        return pl.BlockSpec((tm, tn), lambda i, j: (i, gate0 + g * per_gate + j))

    def w_spec(k):
        return pl.BlockSpec((None, k, tn), lambda i, j: (layer, 0, j))

    return pl.pallas_call(
        _branch_kernel,
        grid=(m // tm, d // tn),
        in_specs=[
            pl.BlockSpec((tm, A_W), lambda i, j: (i, 0)),
            pl.BlockSpec((tm, M_V), lambda i, j: (i, 0)),
            pl.BlockSpec((tm, W_C), lambda i, j: (i, 0)),
            gate_spec(0), gate_spec(1), gate_spec(2),
            w_spec(A_W), w_spec(M_V), w_spec(W_C),
        ],
        out_specs=pl.BlockSpec((tm, tn), lambda i, j: (i, j)),
        out_shape=jax.ShapeDtypeStruct((m, d), BF16),
        compiler_params=_cparams(("parallel", "parallel")),
        name="branch_merge",
    )(a_out, hm, c_out, u_rest, u_rest, u_rest, w_br_a, w_br_m, w_br_c)


ATTN_NB = 8


def _attn_blocks(blocks):
    qi = lax.broadcasted_iota(jnp.int32, (A_BLK, A_BLK), 0)
    ki = lax.broadcasted_iota(jnp.int32, (A_BLK, A_BLK), 1)
    dist = ki - qi
    scale = HD ** -0.5
    n = len(blocks)
    qb = [b[0].astype(BF16) for b in blocks]
    s_cur = [lax.dot_general(qb[i], blocks[i][1].astype(BF16), NT_DIMS, preferred_element_type=F32)
             for i in range(n)]
    s_prev = [lax.dot_general(qb[i], blocks[i][2].astype(BF16), NT_DIMS, preferred_element_type=F32)
              for i in range(n)]
    s_cur = [jnp.where(dist <= 0, s * scale, -jnp.inf) for s in s_cur]
    s_prev = [jnp.where(dist >= blocks[i][5], s_prev[i] * scale, -jnp.inf) for i in range(n)]
    mx = [jnp.maximum(jnp.max(s_cur[i], axis=-1, keepdims=True), jnp.max(s_prev[i], axis=-1, keepdims=True))
          for i in range(n)]
    p_cur = [jnp.exp(s_cur[i] - mx[i]) for i in range(n)]
    p_prev = [jnp.exp(s_prev[i] - mx[i]) for i in range(n)]
    den = [jnp.sum(p_cur[i], axis=-1, keepdims=True) + jnp.sum(p_prev[i], axis=-1, keepdims=True)
           for i in range(n)]
    o = [jnp.dot(p_cur[i].astype(BF16), blocks[i][3].astype(BF16), preferred_element_type=F32)
         + jnp.dot(p_prev[i].astype(BF16), blocks[i][4].astype(BF16), preferred_element_type=F32)
         for i in range(n)]
    return [(o[i] / den[i], mx[i] + jnp.log(den[i])) for i in range(n)]


def _attn_kernel(*refs):
    ins = refs[:5 * N_GROUPS]
    a_ref = refs[5 * N_GROUPS]
    o_scr = refs[5 * N_GROUPS + 1:5 * N_GROUPS + 1 + N_GROUPS]
    l_scr = refs[5 * N_GROUPS + 1 + N_GROUPS:]
    first_lim = jnp.where(pl.program_id(1) > 0, 0, A_BLK)
    for g in range(N_GROUPS):
        d = A_DILATIONS[g]
        q_ref, kc_ref, kp_ref, vc_ref, vp_ref = ins[5 * g:5 * g + 5]
        span = d * A_BLK
        nblk = SEG // span
        nb = min(ATTN_NB, nblk) if nblk > 1 else ATTN_NB

        def rows(start, d=d):
            return pl.ds(start, A_BLK, stride=d) if d > 1 else pl.ds(start, A_BLK)

        def head(r, rows=rows, q_ref=q_ref, kc_ref=kc_ref, kp_ref=kp_ref, vc_ref=vc_ref, vp_ref=vp_ref):
            return (q_ref[rows(r), :], kc_ref[rows(r), :], kp_ref[rows(r), :],
                    vc_ref[rows(r), :], vp_ref[rows(r), :], first_lim)

        def inner(start, rows=rows, span=span, q_ref=q_ref, kc_ref=kc_ref, vc_ref=vc_ref):
            return (q_ref[rows(start), :], kc_ref[rows(start), :], kc_ref[rows(start - span), :],
                    vc_ref[rows(start), :], vc_ref[rows(start - span), :], 0)

        def run(starts, blocks, g=g, rows=rows):
            for start, (o, lse) in zip(starts, _attn_blocks(blocks)):
                o_scr[g][rows(start), :] = o
                l_scr[g][rows(start), :] = jnp.broadcast_to(lse, (A_BLK, LANES))

        if nblk == 1:
            def body(t, c, run=run, head=head):
                rs = [t * nb + i for i in range(nb)]
                run(rs, [head(r) for r in rs])
                return c
            lax.fori_loop(0, d // nb, body, 0)
        else:
            assert nblk % nb == 0

            def body(r, c, run=run, head=head, inner=inner, span=span, nblk=nblk):
                starts = [r + i * span for i in range(nb)]
                run(starts, [head(r)] + [inner(s) for s in starts[1:]])

                def more(t, c2):
                    st = [r + (t * nb + i) * span for i in range(nb)]
                    run(st, [inner(s) for s in st])
                    return c2
                if nblk > nb:
                    lax.fori_loop(1, nblk // nb, more, 0)
                return c
            if d == 1:
                body(0, 0)
            else:
                lax.fori_loop(0, d, body, 0)

    chunk = 64

    def merge(c, carry):
        sl = pl.ds(pl.multiple_of(c * chunk, chunk), chunk)
        l0, l1, l2 = l_scr[0][sl, :], l_scr[1][sl, :], l_scr[2][sl, :]
        mx = jnp.maximum(jnp.maximum(l0, l1), l2)
        e0, e1, e2 = jnp.exp(l0 - mx), jnp.exp(l1 - mx), jnp.exp(l2 - mx)
        den = e0 + e1 + e2
        a = (e0 / den) * o_scr[0][sl, :] + (e1 / den) * o_scr[1][sl, :] + (e2 / den) * o_scr[2][sl, :]
        a_ref[sl, :] = a.astype(a_ref.dtype)
        return carry
    lax.fori_loop(0, SEG // chunk, merge, 0)


def _attn_prompt(u_attn, batch, seq):
    assert seq % SEG == 0
    nseg = seq // SEG
    specs = []
    for g in range(N_GROUPS):
        d = A_DILATIONS[g]
        assert A_WINDOWS[g] // d == A_BLK
        per_seg = SEG // (d * A_BLK)

        def cur(part, g=g):
            return pl.BlockSpec((SEG, HD), lambda b, s, h: (b * nseg + s, (3 * g + part) * N_HEADS + h))

        def prev(part, g=g, d=d, per_seg=per_seg):
            return pl.BlockSpec((d * A_BLK, HD),
                                lambda b, s, h: (jnp.maximum((b * nseg + s) * per_seg - 1, 0),
                                                 (3 * g + part) * N_HEADS + h))
        specs += [cur(0), cur(1), prev(1), cur(2), prev(2)]
    return pl.pallas_call(
        _attn_kernel,
        grid=(batch, nseg, N_HEADS),
        in_specs=specs,
        out_specs=pl.BlockSpec((SEG, HD), lambda b, s, h: (b * nseg + s, h)),
        out_shape=jax.ShapeDtypeStruct((batch * seq, A_W), BF16),
        scratch_shapes=[pltpu.VMEM((SEG, HD), F32)] * (2 * N_GROUPS),
        compiler_params=_cparams(("parallel", "parallel", "parallel")),
        name="attn_prompt",
    )(*([u_attn] * (5 * N_GROUPS)))


def _conv_prompt_kernel(blocks_per_seq, u_ref, b_ref, c_ref, cw_ref, co_ref, st_ref, p_scr):
    tm = u_ref.shape[0]
    i = pl.program_id(0)

    @pl.when(i % blocks_per_seq == 0)
    def _():
        p_scr[0:8, :] = jnp.zeros((8, p_scr.shape[1]), F32)

    @pl.when(i % blocks_per_seq != 0)
    def _():
        p_scr[0:8, :] = p_scr[tm:tm + 8, :]

    p = c_ref[...].astype(F32) * u_ref[...].astype(F32)
    p_scr[8:8 + tm, :] = p
    z = (cw_ref[0:1, :] * p_scr[6:6 + tm, :] + cw_ref[1:2, :] * p_scr[7:7 + tm, :]
         + cw_ref[2:3, :] * p)
    co_ref[...] = (b_ref[...].astype(F32) * z).astype(co_ref.dtype)
    st_ref[...] = p[tm - (CONV_W - 1):tm, :]


def _conv_prompt(u_rest, conv_w, batch, seq, tm=512):
    m = batch * seq
    blocks_per_seq = seq // tm
    return pl.pallas_call(
        functools.partial(_conv_prompt_kernel, blocks_per_seq),
        grid=(m // tm,),
        in_specs=[
            pl.BlockSpec((tm, W_C), lambda i: (i, 0)),
            pl.BlockSpec((tm, W_C), lambda i: (i, 1)),
            pl.BlockSpec((tm, W_C), lambda i: (i, 2)),
            pl.BlockSpec((CONV_W, W_C), lambda i: (0, 0)),
        ],
        out_specs=[
            pl.BlockSpec((tm, W_C), lambda i: (i, 0)),
            pl.BlockSpec((None, CONV_W - 1, W_C), lambda i: (i // blocks_per_seq, 0, 0)),
        ],
        out_shape=[
            jax.ShapeDtypeStruct((m, W_C), BF16),
            jax.ShapeDtypeStruct((batch, CONV_W - 1, W_C), F32),
        ],
        scratch_shapes=[pltpu.VMEM((tm + 8, W_C), F32)],
        compiler_params=_cparams(("arbitrary",)),
        name="conv_prompt",
    )(u_rest, u_rest, u_rest, conv_w)


M_CHUNK = 128


def _log_sigmoid(x):
    return jnp.minimum(x, 0.0) - jnp.log1p(jnp.exp(-jnp.abs(x)))


def _mlstm_kernel(ua_ref, ub_ref, gt_ref, bias_ref, gm_ref, hm_ref, c_ref, n_ref, m_ref):
    L = M_CHUNK

    @pl.when(pl.program_id(1) == 0)
    def _():
        c_ref[...] = jnp.zeros_like(c_ref)
        n_ref[...] = jnp.zeros_like(n_ref)
        m_ref[...] = jnp.zeros_like(m_ref)

    lane = lax.broadcasted_iota(jnp.int32, (L, LANES), 1)
    row = lax.broadcasted_iota(jnp.int32, (L, L), 0)
    col = lax.broadcasted_iota(jnp.int32, (L, L), 1)
    causal = col <= row
    scale = HD ** -0.5

    g = gt_ref[...] + bias_ref[...]
    x = jnp.where(lane < N_HEADS, g, _log_sigmoid(g))
    cum = jnp.dot(causal.astype(F32), x, precision=lax.Precision.HIGHEST,
                  preferred_element_type=F32)
    x_t = x.T
    cum_t = cum.T
    m_all = m_ref[...]
    m_tile = jnp.zeros((1, LANES), F32)
    lane1 = lax.broadcasted_iota(jnp.int32, (1, LANES), 1)
    tn = (((0,), (0,)), ((), ()))
    for h in range(N_HEADS):
        ig_row = x_t[h:h + 1, :]
        ig_col = x[:, h:h + 1]
        b_row = cum_t[N_HEADS + h:N_HEADS + h + 1, :]
        b_col = cum[:, N_HEADS + h:N_HEADS + h + 1]
        m_st = m_all[:, h:h + 1]
        dlog = jnp.where(causal, b_col - b_row + ig_row, -jnp.inf)
        g_col = b_col + m_st
        m_row = jnp.maximum(g_col, jnp.max(dlog, axis=-1, keepdims=True))
        w_intra = jnp.exp(dlog - m_row)
        w_inter = jnp.exp(g_col - m_row)
        q = ua_ref[:, h * HD:(h + 1) * HD]
        k = ua_ref[:, M_QK + h * HD:M_QK + (h + 1) * HD]
        if h < 2:
            v = ua_ref[:, 2 * M_QK + h * DV:2 * M_QK + (h + 1) * DV]
        else:
            v = ub_ref[:, (h - 2) * DV:(h - 1) * DV]
        og = ub_ref[:, 2 * DV + h * DV:2 * DV + (h + 1) * DV]
        c_st = c_ref[h]
        n_st = n_ref[h:h + 1, :]
        s = lax.dot_general(q, k, NT_DIMS, preferred_element_type=F32) * (w_intra * scale)
        num = (jnp.dot(s.astype(BF16), v, preferred_element_type=F32)
               + w_inter * jnp.dot(q, c_st.astype(BF16), preferred_element_type=F32))
        den = (jnp.sum(s, axis=-1, keepdims=True)
               + w_inter * jnp.sum(q.astype(F32) * n_st, axis=-1, keepdims=True))
        hh = num / jnp.maximum(jnp.abs(den), jnp.exp(-m_row))
        hn = hh * lax.rsqrt(jnp.mean(hh * hh, axis=-1, keepdims=True) + EPS) * gm_ref[:, h * DV:(h + 1) * DV]
        hm_ref[:, h * DV:(h + 1) * DV] = (hn * jax.nn.sigmoid(og.astype(F32))).astype(hm_ref.dtype)
        b_last = b_col[L - 1:L, :]
        g_last = b_last + m_st
        wlog = b_last - b_col + ig_col
        m_new = jnp.maximum(g_last, jnp.max(wlog, axis=0, keepdims=True))
        w_s = jnp.exp(wlog - m_new) * scale
        decay = jnp.exp(g_last - m_new)
        kv = lax.dot_general(k, (w_s * v.astype(F32)).astype(BF16), tn, preferred_element_type=F32)
        c_ref[h] = decay * c_st + kv
        n_ref[h:h + 1, :] = decay * n_st + jnp.sum(w_s * k.astype(F32), axis=0, keepdims=True)
        m_tile = jnp.where(lane1 == h, m_new, m_tile)
    m_ref[...] = m_tile


def _mlstm_prompt(u_m, gates, gate_bias, g_mlstm, batch, seq):
    m = batch * seq
    L = M_CHUNK
    nc = seq // L
    half = U_MLSTM // 2
    return pl.pallas_call(
        _mlstm_kernel,
        grid=(batch, nc),
        in_specs=[
            pl.BlockSpec((L, half), lambda b, c: (b * nc + c, 0)),
            pl.BlockSpec((L, half), lambda b, c: (b * nc + c, 1)),
            pl.BlockSpec((L, LANES), lambda b, c: (b * nc + c, 0)),
            pl.BlockSpec((1, LANES), lambda b, c: (0, 0)),
            pl.BlockSpec((1, M_V), lambda b, c: (0, 0)),
        ],
        out_specs=[
            pl.BlockSpec((L, M_V), lambda b, c: (b * nc + c, 0)),
            pl.BlockSpec((None, N_HEADS, HD, DV), lambda b, c: (b, 0, 0, 0)),
            pl.BlockSpec((None, N_HEADS, HD), lambda b, c: (b, 0, 0)),
            pl.BlockSpec((None, 1, LANES), lambda b, c: (b, 0, 0)),
        ],
        out_shape=[
            jax.ShapeDtypeStruct((m, M_V), BF16),
            jax.ShapeDtypeStruct((batch, N_HEADS, HD, DV), F32),
            jax.ShapeDtypeStruct((batch, N_HEADS, HD), F32),
            jax.ShapeDtypeStruct((batch, 1, LANES), F32),
        ],
        compiler_params=_cparams(("parallel", "arbitrary")),
        name="mlstm_prompt",
    )(u_m, u_m, gates, gate_bias, g_mlstm.reshape(1, M_V))


def _mid_sample_kernel(qkv_ref, r_ref, k1_ref, k2_ref, k3_ref, st_ref, cw_ref, a_ref, co_ref, ns_ref):
    scale = HD ** -0.5
    caches = (k1_ref, k2_ref, k3_ref)
    outs, lses = [], []
    for g in range(N_GROUPS):
        base = 3 * g * N_HEADS
        q = qkv_ref[base:base + N_HEADS, :]
        k_new = qkv_ref[base + N_HEADS:base + 2 * N_HEADS, :]
        v_new = qkv_ref[base + 2 * N_HEADS:base + 3 * N_HEADS, :]
        k_c = caches[g][:, 0]
        v_c = caches[g][:, 1]
        s_c = jnp.sum(k_c * q[None], axis=-1, keepdims=True) * scale
        s_n = jnp.sum(k_new * q, axis=-1, keepdims=True) * scale
        mx = jnp.maximum(jnp.max(s_c, axis=0), s_n)
        p_c = jnp.exp(s_c - mx[None])
        p_n = jnp.exp(s_n - mx)
        den = jnp.sum(p_c, axis=0) + p_n
        outs.append((jnp.sum(p_c * v_c, axis=0) + p_n * v_new) / den)
        lses.append(mx + jnp.log(den))
    mx = jnp.maximum(jnp.maximum(lses[0], lses[1]), lses[2])
    e = [jnp.exp(l - mx) for l in lses]
    den = e[0] + e[1] + e[2]
    a = (e[0] / den) * outs[0] + (e[1] / den) * outs[1] + (e[2] / den) * outs[2]
    a_ref[...] = a.astype(a_ref.dtype)
    p = r_ref[:, 2 * W_C:3 * W_C].astype(F32) * r_ref[:, 0:W_C].astype(F32)
    z = cw_ref[0:1, :] * st_ref[0:1, :] + cw_ref[1:2, :] * st_ref[1:2, :] + cw_ref[2:3, :] * p
    co_ref[...] = (r_ref[:, W_C:2 * W_C].astype(F32) * z).astype(co_ref.dtype)
    ns_ref[0:1, :] = st_ref[1:2, :]
    ns_ref[1:2, :] = p


def _mid_sample(u_attn, u_rest, caches, state_conv, conv_w, layer):
    nb = u_attn.shape[0]
    views, cache_specs = [], []
    for g in range(N_GROUPS):
        dil = A_DILATIONS[g]
        c = caches[g]
        assert c.shape[2] == A_WINDOWS[g]
        views.append(c.reshape(c.shape[0], nb, A_BLK, dil, 2, N_HEADS, HD))
        cache_specs.append(pl.BlockSpec((None, None, A_BLK, None, 2, N_HEADS, HD),
                                        lambda b: (layer, b, 0, 0, 0, 0, 0)))
    return pl.pallas_call(
        _mid_sample_kernel,
        grid=(nb,),
        in_specs=[
            pl.BlockSpec((None, 9 * N_HEADS, HD), lambda b: (b, 0, 0)),
            pl.BlockSpec((None, 1, 3 * W_C), lambda b: (b, 0, 0)),
        ] + cache_specs + [
            pl.BlockSpec((None, None, CONV_W - 1, W_C), lambda b: (layer, b, 0, 0)),
            pl.BlockSpec((CONV_W, W_C), lambda b: (0, 0)),
        ],
        out_specs=[
            pl.BlockSpec((None, N_HEADS, HD), lambda b: (b, 0, 0)),
            pl.BlockSpec((None, 1, W_C), lambda b: (b, 0, 0)),
            pl.BlockSpec((None, CONV_W - 1, W_C), lambda b: (b, 0, 0)),
        ],
        out_shape=[
            jax.ShapeDtypeStruct((nb, N_HEADS, HD), BF16),
            jax.ShapeDtypeStruct((nb, 1, W_C), BF16),
            jax.ShapeDtypeStruct((nb, CONV_W - 1, W_C), F32),
        ],
        compiler_params=_cparams(("parallel",)),
        name="mid_sample",
    )(u_attn.reshape(nb, 9 * N_HEADS, HD), u_rest.reshape(nb, 1, -1), *views, state_conv, conv_w)


def _mlstm_step_kernel(u_ref, gt_ref, bias_ref, gm_ref, c_ref, n_ref, m_ref,
                       hm_ref, co_ref, no_ref, mo_ref):
    scale = HD ** -0.5
    g = gt_ref[...] + bias_ref[...]
    lf_all = _log_sigmoid(g)
    m_all = m_ref[...]
    lane1 = lax.broadcasted_iota(jnp.int32, (1, LANES), 1)
    eye = (lax.broadcasted_iota(jnp.int32, (HD, HD), 0)
           == lax.broadcasted_iota(jnp.int32, (HD, HD), 1))
    m_tile = jnp.zeros((1, LANES), F32)
    for h in range(N_HEADS):
        ig = g[:, h:h + 1]
        lf = lf_all[:, N_HEADS + h:N_HEADS + h + 1]
        m_st = m_all[:, h:h + 1]
        q = u_ref[:, h * HD:(h + 1) * HD]
        k = u_ref[:, M_QK + h * HD:M_QK + (h + 1) * HD]
        v = u_ref[:, 2 * M_QK + h * DV:2 * M_QK + (h + 1) * DV]
        og = u_ref[:, 2 * M_QK + M_V + h * DV:2 * M_QK + M_V + (h + 1) * DV]
        qf, kf, vf = q.astype(F32), k.astype(F32), v.astype(F32)
        c_st = c_ref[h]
        n_st = n_ref[h:h + 1, :]
        g_col = lf + m_st
        m_row = jnp.maximum(g_col, ig)
        w_intra = jnp.exp(ig - m_row)
        w_inter = jnp.exp(g_col - m_row)
        s = jnp.sum(qf * kf, axis=-1, keepdims=True) * (w_intra * scale)
        q_rows = jnp.broadcast_to(qf, (16, HD)).astype(BF16)
        qc = jnp.dot(q_rows, c_st.astype(BF16), preferred_element_type=F32)[0:1, :]
        num = s * vf + w_inter * qc
        den = s + w_inter * jnp.sum(qf * n_st, axis=-1, keepdims=True)
        hh = num / jnp.maximum(jnp.abs(den), jnp.exp(-m_row))
        hn = hh * lax.rsqrt(jnp.mean(hh * hh, axis=-1, keepdims=True) + EPS) * gm_ref[:, h * DV:(h + 1) * DV]
        hm_ref[:, h * DV:(h + 1) * DV] = (hn * jax.nn.sigmoid(og.astype(F32))).astype(hm_ref.dtype)
        w_s = w_intra * scale
        k_diag = jnp.where(eye, jnp.broadcast_to(kf, (HD, HD)), 0.0).astype(BF16)
        v_rows = jnp.broadcast_to(w_s * vf, (HD, DV)).astype(BF16)
        kv = jnp.dot(k_diag, v_rows, preferred_element_type=F32)
        co_ref[h] = w_inter * c_st + kv
        no_ref[h:h + 1, :] = w_inter * n_st + w_s * kf
        m_tile = jnp.where(lane1 == h, m_row, m_tile)
    mo_ref[...] = m_tile


def _mlstm_step(u_m, gates, gate_bias, g_mlstm, state_c, state_n, m_pad, layer):
    nb = u_m.shape[0]
    return pl.pallas_call(
        _mlstm_step_kernel,
        grid=(nb,),
        in_specs=[
            pl.BlockSpec((None, 1, U_MLSTM), lambda b: (b, 0, 0)),
            pl.BlockSpec((None, 1, LANES), lambda b: (b, 0, 0)),
            pl.BlockSpec((1, LANES), lambda b: (0, 0)),
            pl.BlockSpec((1, M_V), lambda b: (0, 0)),
            pl.BlockSpec((None, None, N_HEADS, HD, DV), lambda b: (layer, b, 0, 0, 0)),
            pl.BlockSpec((None, None, N_HEADS, HD), lambda b: (layer, b, 0, 0)),
            pl.BlockSpec((None, 1, LANES), lambda b: (b, 0, 0)),
        ],
        out_specs=[
            pl.BlockSpec((None, 1, M_V), lambda b: (b, 0, 0)),
            pl.BlockSpec((None, N_HEADS, HD, DV), lambda b: (b, 0, 0, 0)),
            pl.BlockSpec((None, N_HEADS, HD), lambda b: (b, 0, 0)),
            pl.BlockSpec((None, 1, LANES), lambda b: (b, 0, 0)),
        ],
        out_shape=[
            jax.ShapeDtypeStruct((nb, 1, M_V), BF16),
            jax.ShapeDtypeStruct((nb, N_HEADS, HD, DV), F32),
            jax.ShapeDtypeStruct((nb, N_HEADS, HD), F32),
            jax.ShapeDtypeStruct((nb, 1, LANES), F32),
        ],
        compiler_params=_cparams(("parallel",)),
        name="mlstm_step",
    )(u_m.reshape(nb, 1, U_MLSTM), gates.reshape(nb, 1, LANES), gate_bias,
      g_mlstm.reshape(1, M_V), state_c, state_n, m_pad)


def _pad_lanes(t):
    return jnp.pad(t, [(0, 0)] * (t.ndim - 1) + [(0, LANES - t.shape[-1])])


def _kv_rows(u_attn, group, lead_shape):
    k = u_attn[..., (3 * group + 1) * A_W:(3 * group + 2) * A_W]
    v = u_attn[..., (3 * group + 2) * A_W:(3 * group + 3) * A_W]
    return jnp.stack([k, v], axis=-2).reshape(lead_shape + (2, N_HEADS, HD))


def _run_group(x, mods, states, p, prompt):
    batch, seq, d = x.shape
    m = batch * seq
    depth = p["w_in_t"].shape[0]
    tm = 1024 if prompt else m
    xf = x.reshape(m, d)
    new_states = []
    for l in range(depth):
        sh1, sc1, gt1, sh2, sc2, gt2 = mods[l]
        u_attn, u_m, gates, u_rest = _in_proj(xf, p["g_norm1"][l], sh1, sc1, p["w_in_t"], l, tm)
        gate_bias = p["gate_bias"][l]
        if prompt:
            a_out = _attn_prompt(u_attn, batch, seq)
            c_out, new_conv = _conv_prompt(u_rest, p["conv_w"][l], batch, seq)
            hm, c_n, n_n, m_n = _mlstm_prompt(u_m, gates, gate_bias, p["g_mlstm"][l], batch, seq)
            u3 = u_attn.reshape(batch, seq, U_ATTN)
            bufs = []
            for g in range(N_GROUPS):
                keep = min(A_WINDOWS[g], seq)
                bufs.append(_kv_rows(u3[:, seq - keep:], g, (batch, keep)))
        else:
            caches, st_c, st_n, st_m, st_conv = states
            a_out, c_out, new_conv = _mid_sample(u_attn, u_rest[:, :3 * W_C], caches, st_conv,
                                                 p["conv_w"][l], l)
            a_out = a_out.reshape(m, A_W)
            c_out = c_out.reshape(m, W_C)
            hm, c_n, n_n, m_n = _mlstm_step(u_m, gates, gate_bias, p["g_mlstm"][l],
                                            st_c, st_n, _pad_lanes(st_m[l])[:, None, :], l)
            hm = hm.reshape(m, M_V)
            bufs = [_kv_rows(u_attn, g, (batch, seq)) for g in range(N_GROUPS)]
        merged = _branch_merge(a_out, hm, c_out, u_rest, p["w_br_a"], p["w_br_m"], p["w_br_c"], l, tm)
        xf = _mm_res(merged, p["w_mix_out"], l, xf, gt1, tm)
        act = _ffn_up(xf, p["g_norm2"][l], sh2, sc2, p["w_ffn_gate"], p["w_ffn_up"], l, tm)
        xf = _mm_res(act, p["w_ffn_down"], l, xf, gt2, tm)
        new_states.append(bufs + [c_n, n_n, m_n[:, 0, :N_HEADS], new_conv])
    y = _final_norm(xf, p["g_final"], min(m, 512))
    stacked = [jnp.stack([n[i] for n in new_states], axis=0) for i in range(7)]
    return y.reshape(batch, seq, d), stacked


def kernel(x_prompt, x_sample, cache_a1_kv, cache_a2_kv, cache_a3_kv, state_mlstm_c, state_mlstm_n,
           state_mlstm_m, state_conv, c_prompt, c_sample, w_ada, b_ada, g_norm1, g_norm2, w_in, b_igate,
           b_fgate, g_mlstm, conv_w, w_br_a, w_br_m, w_br_c, w_mix_out, w_ffn_gate, w_ffn_up, w_ffn_down,
           g_final):
    depth, d = g_norm1.shape
    nbp = c_prompt.shape[0]
    nbs = c_sample.shape[0]
    assert w_in.shape[-1] == U_REST_OFF + 3 * W_C + 3 * d

    c_all = jnp.concatenate([c_prompt, c_sample], axis=0)
    c_all = jnp.pad(c_all, ((0, (-c_all.shape[0]) % 8), (0, 0)))
    ada = _ada(c_all, w_ada, b_ada)
    mods_p, mods_s = [], []
    for l in range(depth):
        mods_p.append([ada[l, :nbp, i * d:(i + 1) * d].reshape(nbp, 1, d) for i in range(6)])
        mods_s.append([ada[l, nbp:nbp + nbs, i * d:(i + 1) * d].reshape(1, nbs, d) for i in range(6)])

    params = dict(
        g_norm1=g_norm1, g_norm2=g_norm2, g_mlstm=g_mlstm, conv_w=conv_w,
        w_br_a=w_br_a, w_br_m=w_br_m, w_br_c=w_br_c, w_mix_out=w_mix_out,
        w_ffn_gate=w_ffn_gate, w_ffn_up=w_ffn_up, w_ffn_down=w_ffn_down, g_final=g_final,
        w_in_t=jnp.swapaxes(w_in, 1, 2),
        gate_bias=_pad_lanes(jnp.concatenate([b_igate, b_fgate], axis=-1))[:, None, :],
    )

    y_p, ps = _run_group(x_prompt, mods_p, None, params, True)
    caches = (cache_a1_kv, cache_a2_kv, cache_a3_kv)
    states = (caches, state_mlstm_c, state_mlstm_n, state_mlstm_m, state_conv)
    y_s, ss = _run_group(x_sample, mods_s, states, params, False)
    seq_s = x_sample.shape[1]
    for g in range(N_GROUPS):
        win = caches[g].shape[2]
        keep = [(0, 0, 0)] * 2
        tail = [(0, 0, 0)] * 3
        zero = jnp.zeros((), F32)
        shifted = lax.pad(caches[g], zero, keep + [(-seq_s, seq_s, 0)] + tail)
        ss[g] = shifted + lax.pad(ss[g], zero, keep + [(win - seq_s, 0, 0)] + tail)
    return (y_p, y_s, ps[0], ss[0], ps[1], ss[1], ps[2], ss[2], ps[3], ss[3], ps[4], ss[4],
            ps[5], ss[5], ps[6], ss[6])
```

```python
import functools

import jax
import jax.numpy as jnp
from jax import lax
from jax.experimental import pallas as pl
from jax.experimental.pallas import tpu as pltpu

F32 = jnp.float32
BF16 = jnp.bfloat16
EPS = 1e-6

A_WINDOWS = (128, 512, 2048)
A_DILATIONS = (1, 4, 16)
N_GROUPS = 3
HD = 128
A_BLK = 128
N_HEADS = 4
DV = 256
A_W = N_HEADS * HD
M_QK = N_HEADS * HD
M_V = N_HEADS * DV
W_C = 1024
CONV_W = 3
U_ATTN = 9 * A_W
U_MLSTM = 2 * M_QK + 2 * M_V
U_MAIN = U_ATTN + U_MLSTM
N_GATE_COLS = 2 * N_HEADS
U_REST_OFF = U_MAIN + N_GATE_COLS
LANES = 128
SUBLANES = 8
SEG = A_DILATIONS[-1] * A_BLK

VMEM_LIMIT = 52 * 1024 * 1024
NT_DIMS = (((1,), (1,)), ((), ()))


def _cparams(sem):
    return pltpu.CompilerParams(dimension_semantics=sem, vmem_limit_bytes=VMEM_LIMIT)


def _ada_kernel(c_ref, w_ref, b_ref, o_ref):
    c = c_ref[...]
    a = (c * jax.nn.sigmoid(c)).astype(BF16)
    o_ref[...] = jnp.dot(a, w_ref[...].astype(BF16), preferred_element_type=F32) + b_ref[...]


def _ada(c_all, w_ada, b_ada, tn=1024):
    depth, d, n = w_ada.shape
    r = c_all.shape[0]
    return pl.pallas_call(
        _ada_kernel,
        grid=(depth, n // tn),
        in_specs=[
            pl.BlockSpec((r, d), lambda l, j: (0, 0)),
            pl.BlockSpec((None, d, tn), lambda l, j: (l, 0, j)),
            pl.BlockSpec((None, 1, tn), lambda l, j: (l, 0, j)),
        ],
        out_specs=pl.BlockSpec((None, r, tn), lambda l, j: (l, 0, j)),
        out_shape=jax.ShapeDtypeStruct((depth, r, n), F32),
        compiler_params=_cparams(("parallel", "parallel")),
        name="ada",
    )(c_all, w_ada, b_ada.reshape(depth, 1, n))


def _modulated_norm(x_ref, g_ref, sh_ref, sc_ref, h_scr):
    x = x_ref[...]
    ms = jnp.mean(x * x, axis=-1, keepdims=True)
    y = x * lax.rsqrt(ms + EPS) * g_ref[...]
    h_scr[...] = (y * (1.0 + sc_ref[...]) + sh_ref[...]).astype(BF16)


def _row_specs(m, d, tm, shift):
    groups = shift.shape[0]
    blocks_per_group = (m // groups) // tm
    mod = pl.BlockSpec((None, shift.shape[1], d), lambda i, j: (i // blocks_per_group, 0, 0))
    return [
        pl.BlockSpec((tm, d), lambda i, j: (i, 0)),
        pl.BlockSpec((1, d), lambda i, j: (0, 0)),
        mod,
        mod,
    ]


IN_TN = 512
ATTN_TILES = U_ATTN // IN_TN
MAIN_TILES = U_MAIN // IN_TN


def _in_proj_kernel(x_ref, g_ref, sh_ref, sc_ref, w_ref, wg_ref, wn_ref,
                    oa_ref, om_ref, og_ref, or_ref, h_scr):
    j = pl.program_id(1)

    @pl.when(j == 0)
    def _():
        _modulated_norm(x_ref, g_ref, sh_ref, sc_ref, h_scr)
        wg = jnp.concatenate([wg_ref[...], jnp.zeros((LANES - N_GATE_COLS, wg_ref.shape[1]), F32)], axis=0)
        og_ref[...] = lax.dot_general(h_scr[...], wg.astype(BF16), NT_DIMS, preferred_element_type=F32)

    @pl.when(j < ATTN_TILES)
    def _():
        oa_ref[...] = lax.dot_general(h_scr[...], w_ref[...].astype(BF16), NT_DIMS,
                                      preferred_element_type=F32)

    @pl.when(jnp.logical_and(j >= ATTN_TILES, j < MAIN_TILES))
    def _():
        om_ref[...] = lax.dot_general(h_scr[...], w_ref[...].astype(BF16), NT_DIMS,
                                      preferred_element_type=F32).astype(om_ref.dtype)

    @pl.when(j >= MAIN_TILES)
    def _():
        w = jnp.concatenate([w_ref[N_GATE_COLS:, :], wn_ref[...]], axis=0).astype(BF16)
        or_ref[...] = lax.dot_general(h_scr[...], w, NT_DIMS,
                                      preferred_element_type=F32).astype(or_ref.dtype)


def _in_proj(x, g_norm, shift, scale, w_t, layer, tm):
    m, d = x.shape
    tn = IN_TN
    n_rest = w_t.shape[1] - U_REST_OFF
    n_tiles = MAIN_TILES + n_rest // tn
    assert N_GATE_COLS == SUBLANES and n_rest % tn == 0
    rows_per_tile = tn // SUBLANES
    return pl.pallas_call(
        _in_proj_kernel,
        grid=(m // tm, n_tiles),
        in_specs=_row_specs(m, d, tm, shift) + [
            pl.BlockSpec((None, tn, d), lambda i, j: (layer, j, 0)),
            pl.BlockSpec((None, SUBLANES, d), lambda i, j: (layer, U_MAIN // SUBLANES, 0)),
            pl.BlockSpec((None, SUBLANES, d),
                         lambda i, j: (layer, (jnp.maximum(j, MAIN_TILES) + 1) * rows_per_tile, 0)),
        ],
        out_specs=[
            pl.BlockSpec((tm, tn), lambda i, j: (i, jnp.minimum(j, ATTN_TILES - 1))),
            pl.BlockSpec((tm, tn), lambda i, j: (i, jnp.clip(j - ATTN_TILES, 0, MAIN_TILES - ATTN_TILES - 1))),
            pl.BlockSpec((tm, LANES), lambda i, j: (i, 0)),
            pl.BlockSpec((tm, tn), lambda i, j: (i, jnp.maximum(j - MAIN_TILES, 0))),
        ],
        out_shape=[
            jax.ShapeDtypeStruct((m, U_ATTN), F32),
            jax.ShapeDtypeStruct((m, U_MLSTM), BF16),
            jax.ShapeDtypeStruct((m, LANES), F32),
            jax.ShapeDtypeStruct((m, n_rest), BF16),
        ],
        scratch_shapes=[pltpu.VMEM((tm, d), BF16)],
        compiler_params=_cparams(("parallel", "arbitrary")),
        name="in_proj",
    )(x, g_norm.reshape(1, d), shift, scale, w_t, w_t, w_t)


def _ffn_up_kernel(x_ref, g_ref, sh_ref, sc_ref, wg_ref, wu_ref, o_ref, h_scr):
    @pl.when(pl.program_id(1) == 0)
    def _():
        _modulated_norm(x_ref, g_ref, sh_ref, sc_ref, h_scr)

    h = h_scr[...]
    a = jnp.dot(h, wg_ref[...].astype(BF16), preferred_element_type=F32)
    b = jnp.dot(h, wu_ref[...].astype(BF16), preferred_element_type=F32)
    o_ref[...] = (a * jax.nn.sigmoid(a) * b).astype(o_ref.dtype)


def _ffn_up(x, g_norm, shift, scale, w_gate, w_up, layer, tm, tn=512):
    m, d = x.shape
    d_ff = w_gate.shape[-1]
    wspec = pl.BlockSpec((None, d, tn), lambda i, j: (layer, 0, j))
    return pl.pallas_call(
        _ffn_up_kernel,
        grid=(m // tm, d_ff // tn),
        in_specs=_row_specs(m, d, tm, shift) + [wspec, wspec],
        out_specs=pl.BlockSpec((tm, tn), lambda i, j: (i, j)),
        out_shape=jax.ShapeDtypeStruct((m, d_ff), BF16),
        scratch_shapes=[pltpu.VMEM((tm, d), BF16)],
        compiler_params=_cparams(("parallel", "arbitrary")),
        name="ffn_up",
    )(x, g_norm.reshape(1, d), shift, scale, w_gate, w_up)


def _mm_res_kernel(a_ref, w_ref, res_ref, gt_ref, o_ref):
    y = jnp.dot(a_ref[...], w_ref[...].astype(BF16), preferred_element_type=F32)
    o_ref[...] = res_ref[...] + gt_ref[...] * y


def _mm_res(a, w, layer, res, gate, tm):
    m, kdim = a.shape
    d = res.shape[-1]
    tn = 512 if kdim <= 2048 else 256
    groups = gate.shape[0]
    blocks_per_group = (m // groups) // tm
    return pl.pallas_call(
        _mm_res_kernel,
        grid=(m // tm, d // tn),
        in_specs=[
            pl.BlockSpec((tm, kdim), lambda i, j: (i, 0)),
            pl.BlockSpec((None, kdim, tn), lambda i, j: (layer, 0, j)),
            pl.BlockSpec((tm, tn), lambda i, j: (i, j)),
            pl.BlockSpec((None, gate.shape[1], tn), lambda i, j: (i // blocks_per_group, 0, j)),
        ],
        out_specs=pl.BlockSpec((tm, tn), lambda i, j: (i, j)),
        out_shape=jax.ShapeDtypeStruct((m, d), F32),
        compiler_params=_cparams(("parallel", "parallel")),
        name="mm_res",
    )(a, w, res, gate)


def _final_norm_kernel(x_ref, g_ref, o_ref):
    x = x_ref[...]
    ms = jnp.mean(x * x, axis=-1, keepdims=True)
    o_ref[...] = x * lax.rsqrt(ms + EPS) * g_ref[...]


def _final_norm(x, g_final, tm):
    m, d = x.shape
    return pl.pallas_call(
        _final_norm_kernel,
        grid=(m // tm,),
        in_specs=[pl.BlockSpec((tm, d), lambda i: (i, 0)), pl.BlockSpec((1, d), lambda i: (0, 0))],
        out_specs=pl.BlockSpec((tm, d), lambda i: (i, 0)),
        out_shape=jax.ShapeDtypeStruct((m, d), F32),
        compiler_params=_cparams(("parallel",)),
        name="final_norm",
    )(x, g_final.reshape(1, d))


def _branch_kernel(a_ref, m_ref, c_ref, ga_ref, gm_ref, gc_ref, wa_ref, wm_ref, wc_ref, o_ref):
    def term(x_ref, w_ref, g_ref):
        y = jnp.dot(x_ref[...], w_ref[...].astype(BF16), preferred_element_type=F32)
        return jax.nn.sigmoid(g_ref[...].astype(F32)) * y

    o_ref[...] = (term(a_ref, wa_ref, ga_ref) + term(m_ref, wm_ref, gm_ref)
                  + term(c_ref, wc_ref, gc_ref)).astype(o_ref.dtype)


def _branch_merge(a_out, hm, c_out, u_rest, w_br_a, w_br_m, w_br_c, layer, tm, tn=512):
    m = a_out.shape[0]
    d = w_br_a.shape[-1]
    gate0 = (3 * W_C) // tn
    per_gate = d // tn

    def gate_spec(g):
        return pl.BlockSpec((tm, tn), lambda i, j: (i, gate0 + g * per_gate + j))

    def w_spec(k):
        return pl.BlockSpec((None, k, tn), lambda i, j: (layer, 0, j))

    return pl.pallas_call(
        _branch_kernel,
        grid=(m // tm, d // tn),
        in_specs=[
            pl.BlockSpec((tm, A_W), lambda i, j: (i, 0)),
            pl.BlockSpec((tm, M_V), lambda i, j: (i, 0)),
            pl.BlockSpec((tm, W_C), lambda i, j: (i, 0)),
            gate_spec(0), gate_spec(1), gate_spec(2),
            w_spec(A_W), w_spec(M_V), w_spec(W_C),
        ],
        out_specs=pl.BlockSpec((tm, tn), lambda i, j: (i, j)),
        out_shape=jax.ShapeDtypeStruct((m, d), BF16),
        compiler_params=_cparams(("parallel", "parallel")),
        name="branch_merge",
    )(a_out, hm, c_out, u_rest, u_rest, u_rest, w_br_a, w_br_m, w_br_c)


ATTN_NB = 8


def _attn_blocks(blocks):
    qi = lax.broadcasted_iota(jnp.int32, (A_BLK, A_BLK), 0)
    ki = lax.broadcasted_iota(jnp.int32, (A_BLK, A_BLK), 1)
    dist = ki - qi
    scale = HD ** -0.5
    n = len(blocks)
    qb = [b[0].astype(BF16) for b in blocks]
    s_cur = [lax.dot_general(qb[i], blocks[i][1].astype(BF16), NT_DIMS, preferred_element_type=F32)
             for i in range(n)]
    s_prev = [lax.dot_general(qb[i], blocks[i][2].astype(BF16), NT_DIMS, preferred_element_type=F32)
              for i in range(n)]
    s_cur = [jnp.where(dist <= 0, s * scale, -jnp.inf) for s in s_cur]
    s_prev = [jnp.where(dist >= blocks[i][5], s_prev[i] * scale, -jnp.inf) for i in range(n)]
    mx = [jnp.maximum(jnp.max(s_cur[i], axis=-1, keepdims=True), jnp.max(s_prev[i], axis=-1, keepdims=True))
          for i in range(n)]
    p_cur = [jnp.exp(s_cur[i] - mx[i]) for i in range(n)]
    p_prev = [jnp.exp(s_prev[i] - mx[i]) for i in range(n)]
    den = [jnp.sum(p_cur[i], axis=-1, keepdims=True) + jnp.sum(p_prev[i], axis=-1, keepdims=True)
           for i in range(n)]
    o = [jnp.dot(p_cur[i].astype(BF16), blocks[i][3].astype(BF16), preferred_element_type=F32)
         + jnp.dot(p_prev[i].astype(BF16), blocks[i][4].astype(BF16), preferred_element_type=F32)
         for i in range(n)]
    return [(o[i] / den[i], mx[i] + jnp.log(den[i])) for i in range(n)]


def _attn_kernel(*refs):
    ins = refs[:5 * N_GROUPS]
    a_ref = refs[5 * N_GROUPS]
    o_scr = refs[5 * N_GROUPS + 1:5 * N_GROUPS + 1 + N_GROUPS]
    l_scr = refs[5 * N_GROUPS + 1 + N_GROUPS:]
    first_lim = jnp.where(pl.program_id(1) > 0, 0, A_BLK)
    for g in range(N_GROUPS):
        d = A_DILATIONS[g]
        q_ref, kc_ref, kp_ref, vc_ref, vp_ref = ins[5 * g:5 * g + 5]
        span = d * A_BLK
        nblk = SEG // span
        nb = min(ATTN_NB, nblk) if nblk > 1 else ATTN_NB

        def rows(start, d=d):
            return pl.ds(start, A_BLK, stride=d) if d > 1 else pl.ds(start, A_BLK)

        def head(r, rows=rows, q_ref=q_ref, kc_ref=kc_ref, kp_ref=kp_ref, vc_ref=vc_ref, vp_ref=vp_ref):
            return (q_ref[rows(r), :], kc_ref[rows(r), :], kp_ref[rows(r), :],
                    vc_ref[rows(r), :], vp_ref[rows(r), :], first_lim)

        def inner(start, rows=rows, span=span, q_ref=q_ref, kc_ref=kc_ref, vc_ref=vc_ref):
            return (q_ref[rows(start), :], kc_ref[rows(start), :], kc_ref[rows(start - span), :],
                    vc_ref[rows(start), :], vc_ref[rows(start - span), :], 0)

        def run(starts, blocks, g=g, rows=rows):
            for start, (o, lse) in zip(starts, _attn_blocks(blocks)):
                o_scr[g][rows(start), :] = o
                l_scr[g][rows(start), :] = jnp.broadcast_to(lse, (A_BLK, LANES))

        if nblk == 1:
            def body(t, c, run=run, head=head):
                rs = [t * nb + i for i in range(nb)]
                run(rs, [head(r) for r in rs])
                return c
            lax.fori_loop(0, d // nb, body, 0)
        else:
            assert nblk % nb == 0

            def body(r, c, run=run, head=head, inner=inner, span=span, nblk=nblk):
                starts = [r + i * span for i in range(nb)]
                run(starts, [head(r)] + [inner(s) for s in starts[1:]])

                def more(t, c2):
                    st = [r + (t * nb + i) * span for i in range(nb)]
                    run(st, [inner(s) for s in st])
                    return c2
                if nblk > nb:
                    lax.fori_loop(1, nblk // nb, more, 0)
                return c
            if d == 1:
                body(0, 0)
            else:
                lax.fori_loop(0, d, body, 0)

    chunk = 64

    def merge(c, carry):
        sl = pl.ds(pl.multiple_of(c * chunk, chunk), chunk)
        l0, l1, l2 = l_scr[0][sl, :], l_scr[1][sl, :], l_scr[2][sl, :]
        mx = jnp.maximum(jnp.maximum(l0, l1), l2)
        e0, e1, e2 = jnp.exp(l0 - mx), jnp.exp(l1 - mx), jnp.exp(l2 - mx)
        den = e0 + e1 + e2
        a = (e0 / den) * o_scr[0][sl, :] + (e1 / den) * o_scr[1][sl, :] + (e2 / den) * o_scr[2][sl, :]
        a_ref[sl, :] = a.astype(a_ref.dtype)
        return carry
    lax.fori_loop(0, SEG // chunk, merge, 0)


def _attn_prompt(u_attn, batch, seq):
    assert seq % SEG == 0
    nseg = seq // SEG
    specs = []
    for g in range(N_GROUPS):
        d = A_DILATIONS[g]
        assert A_WINDOWS[g] // d == A_BLK
        per_seg = SEG // (d * A_BLK)

        def cur(part, g=g):
            return pl.BlockSpec((SEG, HD), lambda b, s, h: (b * nseg + s, (3 * g + part) * N_HEADS + h))

        def prev(part, g=g, d=d, per_seg=per_seg):
            return pl.BlockSpec((d * A_BLK, HD),
                                lambda b, s, h: (jnp.maximum((b * nseg + s) * per_seg - 1, 0),
                                                 (3 * g + part) * N_HEADS + h))
        specs += [cur(0), cur(1), prev(1), cur(2), prev(2)]
    return pl.pallas_call(
        _attn_kernel,
        grid=(batch, nseg, N_HEADS),
        in_specs=specs,
        out_specs=pl.BlockSpec((SEG, HD), lambda b, s, h: (b * nseg + s, h)),
        out_shape=jax.ShapeDtypeStruct((batch * seq, A_W), BF16),
        scratch_shapes=[pltpu.VMEM((SEG, HD), F32)] * (2 * N_GROUPS),
        compiler_params=_cparams(("parallel", "parallel", "parallel")),
        name="attn_prompt",
    )(*([u_attn] * (5 * N_GROUPS)))


def _conv_prompt_kernel(blocks_per_seq, u_ref, b_ref, c_ref, cw_ref, co_ref, st_ref, p_scr):
    tm = u_ref.shape[0]
    i = pl.program_id(0)

    @pl.when(i % blocks_per_seq == 0)
    def _():
        p_scr[0:8, :] = jnp.zeros((8, p_scr.shape[1]), F32)

    @pl.when(i % blocks_per_seq != 0)
    def _():
        p_scr[0:8, :] = p_scr[tm:tm + 8, :]

    p = c_ref[...].astype(F32) * u_ref[...].astype(F32)
    p_scr[8:8 + tm, :] = p
    z = (cw_ref[0:1, :] * p_scr[6:6 + tm, :] + cw_ref[1:2, :] * p_scr[7:7 + tm, :]
         + cw_ref[2:3, :] * p)
    co_ref[...] = (b_ref[...].astype(F32) * z).astype(co_ref.dtype)
    st_ref[...] = p[tm - (CONV_W - 1):tm, :]


def _conv_prompt(u_rest, conv_w, batch, seq, tm=512):
    m = batch * seq
    blocks_per_seq = seq // tm
    return pl.pallas_call(
        functools.partial(_conv_prompt_kernel, blocks_per_seq),
        grid=(m // tm,),
        in_specs=[
            pl.BlockSpec((tm, W_C), lambda i: (i, 0)),
            pl.BlockSpec((tm, W_C), lambda i: (i, 1)),
            pl.BlockSpec((tm, W_C), lambda i: (i, 2)),
            pl.BlockSpec((CONV_W, W_C), lambda i: (0, 0)),
        ],
        out_specs=[
            pl.BlockSpec((tm, W_C), lambda i: (i, 0)),
            pl.BlockSpec((None, CONV_W - 1, W_C), lambda i: (i // blocks_per_seq, 0, 0)),
        ],
        out_shape=[
            jax.ShapeDtypeStruct((m, W_C), BF16),
            jax.ShapeDtypeStruct((batch, CONV_W - 1, W_C), F32),
        ],
        scratch_shapes=[pltpu.VMEM((tm + 8, W_C), F32)],
        compiler_params=_cparams(("arbitrary",)),
        name="conv_prompt",
    )(u_rest, u_rest, u_rest, conv_w)


M_CHUNK = 128


def _log_sigmoid(x):
    return jnp.minimum(x, 0.0) - jnp.log1p(jnp.exp(-jnp.abs(x)))


def _mlstm_kernel(ua_ref, ub_ref, gt_ref, bias_ref, gm_ref, hm_ref, c_ref, n_ref, m_ref):
    for b in range(ua_ref.shape[0]):
        _mlstm_chunk(ua_ref.at[b], ub_ref.at[b], gt_ref.at[b], bias_ref, gm_ref,
                     hm_ref.at[b], c_ref.at[b], n_ref.at[b], m_ref.at[b])


def _mlstm_chunk(ua_ref, ub_ref, gt_ref, bias_ref, gm_ref, hm_ref, c_ref, n_ref, m_ref):
    L = M_CHUNK

    @pl.when(pl.program_id(0) == 0)
    def _():
        c_ref[...] = jnp.zeros_like(c_ref)
        n_ref[...] = jnp.zeros_like(n_ref)
        m_ref[...] = jnp.zeros_like(m_ref)

    lane = lax.broadcasted_iota(jnp.int32, (L, LANES), 1)
    row = lax.broadcasted_iota(jnp.int32, (L, L), 0)
    col = lax.broadcasted_iota(jnp.int32, (L, L), 1)
    causal = col <= row
    scale = HD ** -0.5

    g = gt_ref[...] + bias_ref[...]
    x = jnp.where(lane < N_HEADS, g, _log_sigmoid(g))
    cum = jnp.dot(causal.astype(F32), x, precision=lax.Precision.HIGHEST,
                  preferred_element_type=F32)
    x_t = x.T
    cum_t = cum.T
    m_all = m_ref[...]
    m_tile = jnp.zeros((1, LANES), F32)
    lane1 = lax.broadcasted_iota(jnp.int32, (1, LANES), 1)
    tn = (((0,), (0,)), ((), ()))
    for h in range(N_HEADS):
        ig_row = x_t[h:h + 1, :]
        ig_col = x[:, h:h + 1]
        b_row = cum_t[N_HEADS + h:N_HEADS + h + 1, :]
        b_col = cum[:, N_HEADS + h:N_HEADS + h + 1]
        m_st = m_all[:, h:h + 1]
        dlog = jnp.where(causal, b_col - b_row + ig_row, -jnp.inf)
        g_col = b_col + m_st
        m_row = jnp.maximum(g_col, jnp.max(dlog, axis=-1, keepdims=True))
        w_intra = jnp.exp(dlog - m_row)
        w_inter = jnp.exp(g_col - m_row)
        q = ua_ref[:, h * HD:(h + 1) * HD]
        k = ua_ref[:, M_QK + h * HD:M_QK + (h + 1) * HD]
        if h < 2:
            v = ua_ref[:, 2 * M_QK + h * DV:2 * M_QK + (h + 1) * DV]
        else:
            v = ub_ref[:, (h - 2) * DV:(h - 1) * DV]
        og = ub_ref[:, 2 * DV + h * DV:2 * DV + (h + 1) * DV]
        c_st = c_ref[h]
        n_st = n_ref[h:h + 1, :]
        s = lax.dot_general(q, k, NT_DIMS, preferred_element_type=F32) * (w_intra * scale)
        num = (jnp.dot(s.astype(BF16), v, preferred_element_type=F32)
               + w_inter * jnp.dot(q, c_st.astype(BF16), preferred_element_type=F32))
        den = (jnp.sum(s, axis=-1, keepdims=True)
               + w_inter * jnp.sum(q.astype(F32) * n_st, axis=-1, keepdims=True))
        hh = num / jnp.maximum(jnp.abs(den), jnp.exp(-m_row))
        hn = hh * lax.rsqrt(jnp.mean(hh * hh, axis=-1, keepdims=True) + EPS) * gm_ref[:, h * DV:(h + 1) * DV]
        hm_ref[:, h * DV:(h + 1) * DV] = (hn * jax.nn.sigmoid(og.astype(F32))).astype(hm_ref.dtype)
        b_last = b_col[L - 1:L, :]
        g_last = b_last + m_st
        wlog = b_last - b_col + ig_col
        m_new = jnp.maximum(g_last, jnp.max(wlog, axis=0, keepdims=True))
        w_s = jnp.exp(wlog - m_new) * scale
        decay = jnp.exp(g_last - m_new)
        kv = lax.dot_general(k, (w_s * v.astype(F32)).astype(BF16), tn, preferred_element_type=F32)
        c_ref[h] = decay * c_st + kv
        n_ref[h:h + 1, :] = decay * n_st + jnp.sum(w_s * k.astype(F32), axis=0, keepdims=True)
        m_tile = jnp.where(lane1 == h, m_new, m_tile)
    m_ref[...] = m_tile


def _mlstm_prompt(u_m, gates, gate_bias, g_mlstm, batch, seq):
    m = batch * seq
    L = M_CHUNK
    nc = seq // L
    half = U_MLSTM // 2
    u3 = u_m.reshape(batch, seq, U_MLSTM)
    hm, c_n, n_n, m_n = pl.pallas_call(
        _mlstm_kernel,
        grid=(nc,),
        in_specs=[
            pl.BlockSpec((batch, L, half), lambda c: (0, c, 0)),
            pl.BlockSpec((batch, L, half), lambda c: (0, c, 1)),
            pl.BlockSpec((batch, L, LANES), lambda c: (0, c, 0)),
            pl.BlockSpec((1, LANES), lambda c: (0, 0)),
            pl.BlockSpec((1, M_V), lambda c: (0, 0)),
        ],
        out_specs=[
            pl.BlockSpec((batch, L, M_V), lambda c: (0, c, 0)),
            pl.BlockSpec((batch, N_HEADS, HD, DV), lambda c: (0, 0, 0, 0)),
            pl.BlockSpec((batch, N_HEADS, HD), lambda c: (0, 0, 0)),
            pl.BlockSpec((batch, 1, LANES), lambda c: (0, 0, 0)),
        ],
        out_shape=[
            jax.ShapeDtypeStruct((batch, seq, M_V), BF16),
            jax.ShapeDtypeStruct((batch, N_HEADS, HD, DV), F32),
            jax.ShapeDtypeStruct((batch, N_HEADS, HD), F32),
            jax.ShapeDtypeStruct((batch, 1, LANES), F32),
        ],
        compiler_params=_cparams(("arbitrary",)),
        name="mlstm_prompt",
    )(u3, u3, gates.reshape(batch, seq, LANES), gate_bias, g_mlstm.reshape(1, M_V))
    return hm.reshape(m, M_V), c_n, n_n, m_n


def _mid_sample_kernel(qkv_ref, r_ref, k1_ref, k2_ref, k3_ref, st_ref, cw_ref, a_ref, co_ref, ns_ref):
    scale = HD ** -0.5
    caches = (k1_ref, k2_ref, k3_ref)
    outs, lses = [], []
    for g in range(N_GROUPS):
        base = 3 * g * N_HEADS
        q = qkv_ref[base:base + N_HEADS, :]
        k_new = qkv_ref[base + N_HEADS:base + 2 * N_HEADS, :]
        v_new = qkv_ref[base + 2 * N_HEADS:base + 3 * N_HEADS, :]
        k_c = caches[g][:, 0]
        v_c = caches[g][:, 1]
        s_c = jnp.sum(k_c * q[None], axis=-1, keepdims=True) * scale
        s_n = jnp.sum(k_new * q, axis=-1, keepdims=True) * scale
        mx = jnp.maximum(jnp.max(s_c, axis=0), s_n)
        p_c = jnp.exp(s_c - mx[None])
        p_n = jnp.exp(s_n - mx)
        den = jnp.sum(p_c, axis=0) + p_n
        outs.append((jnp.sum(p_c * v_c, axis=0) + p_n * v_new) / den)
        lses.append(mx + jnp.log(den))
    mx = jnp.maximum(jnp.maximum(lses[0], lses[1]), lses[2])
    e = [jnp.exp(l - mx) for l in lses]
    den = e[0] + e[1] + e[2]
    a = (e[0] / den) * outs[0] + (e[1] / den) * outs[1] + (e[2] / den) * outs[2]
    a_ref[...] = a.astype(a_ref.dtype)
    p = r_ref[:, 2 * W_C:3 * W_C].astype(F32) * r_ref[:, 0:W_C].astype(F32)
    z = cw_ref[0:1, :] * st_ref[0:1, :] + cw_ref[1:2, :] * st_ref[1:2, :] + cw_ref[2:3, :] * p
    co_ref[...] = (r_ref[:, W_C:2 * W_C].astype(F32) * z).astype(co_ref.dtype)
    ns_ref[0:1, :] = st_ref[1:2, :]
    ns_ref[1:2, :] = p


def _mid_sample(u_attn, u_rest, caches, state_conv, conv_w, layer):
    nb = u_attn.shape[0]
    views, cache_specs = [], []
    for g in range(N_GROUPS):
        dil = A_DILATIONS[g]
        c = caches[g]
        assert c.shape[2] == A_WINDOWS[g]
        views.append(c.reshape(c.shape[0], nb, A_BLK, dil, 2, N_HEADS, HD))
        cache_specs.append(pl.BlockSpec((None, None, A_BLK, None, 2, N_HEADS, HD),
                                        lambda b: (layer, b, 0, 0, 0, 0, 0)))
    return pl.pallas_call(
        _mid_sample_kernel,
        grid=(nb,),
        in_specs=[
            pl.BlockSpec((None, 9 * N_HEADS, HD), lambda b: (b, 0, 0)),
            pl.BlockSpec((None, 1, 3 * W_C), lambda b: (b, 0, 0)),
        ] + cache_specs + [
            pl.BlockSpec((None, None, CONV_W - 1, W_C), lambda b: (layer, b, 0, 0)),
            pl.BlockSpec((CONV_W, W_C), lambda b: (0, 0)),
        ],
        out_specs=[
            pl.BlockSpec((None, N_HEADS, HD), lambda b: (b, 0, 0)),
            pl.BlockSpec((None, 1, W_C), lambda b: (b, 0, 0)),
            pl.BlockSpec((None, CONV_W - 1, W_C), lambda b: (b, 0, 0)),
        ],
        out_shape=[
            jax.ShapeDtypeStruct((nb, N_HEADS, HD), BF16),
            jax.ShapeDtypeStruct((nb, 1, W_C), BF16),
            jax.ShapeDtypeStruct((nb, CONV_W - 1, W_C), F32),
        ],
        compiler_params=_cparams(("parallel",)),
        name="mid_sample",
    )(u_attn.reshape(nb, 9 * N_HEADS, HD), u_rest.reshape(nb, 1, -1), *views, state_conv, conv_w)


def _mlstm_step_kernel(u_ref, gt_ref, bias_ref, gm_ref, c_ref, n_ref, m_ref,
                       hm_ref, co_ref, no_ref, mo_ref):
    scale = HD ** -0.5
    g = gt_ref[...] + bias_ref[...]
    lf_all = _log_sigmoid(g)
    m_all = m_ref[...]
    lane1 = lax.broadcasted_iota(jnp.int32, (1, LANES), 1)
    eye = (lax.broadcasted_iota(jnp.int32, (HD, HD), 0)
           == lax.broadcasted_iota(jnp.int32, (HD, HD), 1))
    m_tile = jnp.zeros((1, LANES), F32)
    for h in range(N_HEADS):
        ig = g[:, h:h + 1]
        lf = lf_all[:, N_HEADS + h:N_HEADS + h + 1]
        m_st = m_all[:, h:h + 1]
        q = u_ref[:, h * HD:(h + 1) * HD]
        k = u_ref[:, M_QK + h * HD:M_QK + (h + 1) * HD]
        v = u_ref[:, 2 * M_QK + h * DV:2 * M_QK + (h + 1) * DV]
        og = u_ref[:, 2 * M_QK + M_V + h * DV:2 * M_QK + M_V + (h + 1) * DV]
        qf, kf, vf = q.astype(F32), k.astype(F32), v.astype(F32)
        c_st = c_ref[h]
        n_st = n_ref[h:h + 1, :]
        g_col = lf + m_st
        m_row = jnp.maximum(g_col, ig)
        w_intra = jnp.exp(ig - m_row)
        w_inter = jnp.exp(g_col - m_row)
        s = jnp.sum(qf * kf, axis=-1, keepdims=True) * (w_intra * scale)
        q_rows = jnp.broadcast_to(qf, (16, HD)).astype(BF16)
        qc = jnp.dot(q_rows, c_st.astype(BF16), preferred_element_type=F32)[0:1, :]
        num = s * vf + w_inter * qc
        den = s + w_inter * jnp.sum(qf * n_st, axis=-1, keepdims=True)
        hh = num / jnp.maximum(jnp.abs(den), jnp.exp(-m_row))
        hn = hh * lax.rsqrt(jnp.mean(hh * hh, axis=-1, keepdims=True) + EPS) * gm_ref[:, h * DV:(h + 1) * DV]
        hm_ref[:, h * DV:(h + 1) * DV] = (hn * jax.nn.sigmoid(og.astype(F32))).astype(hm_ref.dtype)
        w_s = w_intra * scale
        k_diag = jnp.where(eye, jnp.broadcast_to(kf, (HD, HD)), 0.0).astype(BF16)
        v_rows = jnp.broadcast_to(w_s * vf, (HD, DV)).astype(BF16)
        kv = jnp.dot(k_diag, v_rows, preferred_element_type=F32)
        co_ref[h] = w_inter * c_st + kv
        no_ref[h:h + 1, :] = w_inter * n_st + w_s * kf
        m_tile = jnp.where(lane1 == h, m_row, m_tile)
    mo_ref[...] = m_tile


def _mlstm_step(u_m, gates, gate_bias, g_mlstm, state_c, state_n, m_pad, layer):
    nb = u_m.shape[0]
    return pl.pallas_call(
        _mlstm_step_kernel,
        grid=(nb,),
        in_specs=[
            pl.BlockSpec((None, 1, U_MLSTM), lambda b: (b, 0, 0)),
            pl.BlockSpec((None, 1, LANES), lambda b: (b, 0, 0)),
            pl.BlockSpec((1, LANES), lambda b: (0, 0)),
            pl.BlockSpec((1, M_V), lambda b: (0, 0)),
            pl.BlockSpec((None, None, N_HEADS, HD, DV), lambda b: (layer, b, 0, 0, 0)),
            pl.BlockSpec((None, None, N_HEADS, HD), lambda b: (layer, b, 0, 0)),
            pl.BlockSpec((None, 1, LANES), lambda b: (b, 0, 0)),
        ],
        out_specs=[
            pl.BlockSpec((None, 1, M_V), lambda b: (b, 0, 0)),
            pl.BlockSpec((None, N_HEADS, HD, DV), lambda b: (b, 0, 0, 0)),
            pl.BlockSpec((None, N_HEADS, HD), lambda b: (b, 0, 0)),
            pl.BlockSpec((None, 1, LANES), lambda b: (b, 0, 0)),
        ],
        out_shape=[
            jax.ShapeDtypeStruct((nb, 1, M_V), BF16),
            jax.ShapeDtypeStruct((nb, N_HEADS, HD, DV), F32),
            jax.ShapeDtypeStruct((nb, N_HEADS, HD), F32),
            jax.ShapeDtypeStruct((nb, 1, LANES), F32),
        ],
        compiler_params=_cparams(("parallel",)),
        name="mlstm_step",
    )(u_m.reshape(nb, 1, U_MLSTM), gates.reshape(nb, 1, LANES), gate_bias,
      g_mlstm.reshape(1, M_V), state_c, state_n, m_pad)


def _pad_lanes(t):
    return jnp.pad(t, [(0, 0)] * (t.ndim - 1) + [(0, LANES - t.shape[-1])])


def _kv_rows(u_attn, group, lead_shape):
    k = u_attn[..., (3 * group + 1) * A_W:(3 * group + 2) * A_W]
    v = u_attn[..., (3 * group + 2) * A_W:(3 * group + 3) * A_W]
    return jnp.stack([k, v], axis=-2).reshape(lead_shape + (2, N_HEADS, HD))


def _run_group(x, mods, states, p, prompt):
    batch, seq, d = x.shape
    m = batch * seq
    depth = p["w_in_t"].shape[0]
    tm = 1024 if prompt else m
    xf = x.reshape(m, d)
    new_states = []
    for l in range(depth):
        sh1, sc1, gt1, sh2, sc2, gt2 = mods[l]
        u_attn, u_m, gates, u_rest = _in_proj(xf, p["g_norm1"][l], sh1, sc1, p["w_in_t"], l, tm)
        gate_bias = p["gate_bias"][l]
        if prompt:
            a_out = _attn_prompt(u_attn, batch, seq)
            c_out, new_conv = _conv_prompt(u_rest, p["conv_w"][l], batch, seq)
            hm, c_n, n_n, m_n = _mlstm_prompt(u_m, gates, gate_bias, p["g_mlstm"][l], batch, seq)
            u3 = u_attn.reshape(batch, seq, U_ATTN)
            bufs = []
            for g in range(N_GROUPS):
                keep = min(A_WINDOWS[g], seq)
                bufs.append(_kv_rows(u3[:, seq - keep:], g, (batch, keep)))
        else:
            caches, st_c, st_n, st_m, st_conv = states
            a_out, c_out, new_conv = _mid_sample(u_attn, u_rest[:, :3 * W_C], caches, st_conv,
                                                 p["conv_w"][l], l)
            a_out = a_out.reshape(m, A_W)
            c_out = c_out.reshape(m, W_C)
            hm, c_n, n_n, m_n = _mlstm_step(u_m, gates, gate_bias, p["g_mlstm"][l],
                                            st_c, st_n, _pad_lanes(st_m[l])[:, None, :], l)
            hm = hm.reshape(m, M_V)
            bufs = [_kv_rows(u_attn, g, (batch, seq)) for g in range(N_GROUPS)]
        merged = _branch_merge(a_out, hm, c_out, u_rest, p["w_br_a"], p["w_br_m"], p["w_br_c"], l, tm)
        xf = _mm_res(merged, p["w_mix_out"], l, xf, gt1, tm)
        act = _ffn_up(xf, p["g_norm2"][l], sh2, sc2, p["w_ffn_gate"], p["w_ffn_up"], l, tm)
        xf = _mm_res(act, p["w_ffn_down"], l, xf, gt2, tm)
        new_states.append(bufs + [c_n, n_n, m_n[:, 0, :N_HEADS], new_conv])
    y = _final_norm(xf, p["g_final"], min(m, 512))
    stacked = [jnp.stack([n[i] for n in new_states], axis=0) for i in range(7)]
    return y.reshape(batch, seq, d), stacked


def kernel(x_prompt, x_sample, cache_a1_kv, cache_a2_kv, cache_a3_kv, state_mlstm_c, state_mlstm_n,
           state_mlstm_m, state_conv, c_prompt, c_sample, w_ada, b_ada, g_norm1, g_norm2, w_in, b_igate,
           b_fgate, g_mlstm, conv_w, w_br_a, w_br_m, w_br_c, w_mix_out, w_ffn_gate, w_ffn_up, w_ffn_down,
           g_final):
    depth, d = g_norm1.shape
    nbp = c_prompt.shape[0]
    nbs = c_sample.shape[0]
    assert w_in.shape[-1] == U_REST_OFF + 3 * W_C + 3 * d

    c_all = jnp.concatenate([c_prompt, c_sample], axis=0)
    c_all = jnp.pad(c_all, ((0, (-c_all.shape[0]) % 8), (0, 0)))
    ada = _ada(c_all, w_ada, b_ada)
    mods_p, mods_s = [], []
    for l in range(depth):
        mods_p.append([ada[l, :nbp, i * d:(i + 1) * d].reshape(nbp, 1, d) for i in range(6)])
        mods_s.append([ada[l, nbp:nbp + nbs, i * d:(i + 1) * d].reshape(1, nbs, d) for i in range(6)])

    params = dict(
        g_norm1=g_norm1, g_norm2=g_norm2, g_mlstm=g_mlstm, conv_w=conv_w,
        w_br_a=w_br_a, w_br_m=w_br_m, w_br_c=w_br_c, w_mix_out=w_mix_out,
        w_ffn_gate=w_ffn_gate, w_ffn_up=w_ffn_up, w_ffn_down=w_ffn_down, g_final=g_final,
        w_in_t=jnp.swapaxes(w_in, 1, 2),
        gate_bias=_pad_lanes(jnp.concatenate([b_igate, b_fgate], axis=-1))[:, None, :],
    )

    y_p, ps = _run_group(x_prompt, mods_p, None, params, True)
    caches = (cache_a1_kv, cache_a2_kv, cache_a3_kv)
    states = (caches, state_mlstm_c, state_mlstm_n, state_mlstm_m, state_conv)
    y_s, ss = _run_group(x_sample, mods_s, states, params, False)
    seq_s = x_sample.shape[1]
    for g in range(N_GROUPS):
        win = caches[g].shape[2]
        keep = [(0, 0, 0)] * 2
        tail = [(0, 0, 0)] * 3
        zero = jnp.zeros((), F32)
        shifted = lax.pad(caches[g], zero, keep + [(-seq_s, seq_s, 0)] + tail)
        ss[g] = shifted + lax.pad(ss[g], zero, keep + [(win - seq_s, 0, 0)] + tail)
    return (y_p, y_s, ps[0], ss[0], ps[1], ss[1], ps[2], ss[2], ps[3], ss[3], ps[4], ss[4],
            ps[5], ss[5], ps[6], ss[6])
```
